```python
import math
import jax, jax.numpy as jnp
from jax import lax
import numpy as np

D_MODEL = 1024
BATCH = 2
SEQ = 8192
DEPTH = 4

N_MIXERS = 4
EPS = 1e-6
LN_EPS = 1e-5
CHUNK = 128
D_FF = 4 * D_MODEL

GM_WIDTH = 2 * D_MODEL
GM_GROUPS = 8
GM_GDIM = GM_WIDTH // GM_GROUPS

S5_WIDTH = D_MODEL
S5_GROUP = 16
S5_GROUPS = S5_WIDTH // S5_GROUP
S5_STATE = 64
S5_DT_MIN = 1e-3
S5_DT_MAX = 1e-1

RET_HEADS = max(4, D_MODEL // 256)
RET_DK = D_MODEL // RET_HEADS
RET_DV = 2 * RET_DK
RET_QD = RET_HEADS * RET_DK
RET_VD = RET_HEADS * RET_DV
ROPE_BASE = 10000.0

SC_WIDTH = D_MODEL
CONV_WIDTH = 3

kernel_name = "interleaved_hybrid_gmlp_s5_retnet_shortconv"


def _n_uses(m):
    return len(range(m, DEPTH, N_MIXERS))


def rms_norm(x, g):
    xf = x.astype(jnp.float32)
    y = xf * lax.rsqrt(jnp.mean(xf * xf, axis=-1, keepdims=True) + EPS)
    return (y * g.astype(jnp.float32)).astype(x.dtype)


def gmlp_mixer(h, w_in, ln_g, w_s, b_s, w_out):
    b, l, _ = h.shape
    z = jax.nn.gelu(h @ w_in)
    u, v = jnp.split(z, 2, axis=-1)
    vf = v.astype(jnp.float32)
    mu = jnp.mean(vf, axis=-1, keepdims=True)
    var = jnp.mean(jnp.square(vf - mu), axis=-1, keepdims=True)
    v = ((vf - mu) * lax.rsqrt(var + LN_EPS) * ln_g.astype(jnp.float32)).astype(h.dtype)
    v = v.reshape(b, l // CHUNK, CHUNK, GM_GROUPS, GM_GDIM)
    causal = jnp.tril(jnp.ones((CHUNK, CHUNK), dtype=bool))
    ws = jnp.where(causal[None], w_s, jnp.zeros_like(w_s))
    sv = jnp.einsum('gts,bnsge->bntge', ws, v) + b_s.T[None, None, :, :, None]
    return (u * sv.reshape(b, l, GM_WIDTH)) @ w_out


def _complex_combine(e1, e2):
    a1r, a1i, b1r, b1i = e1
    a2r, a2i, b2r, b2i = e2
    return (a2r * a1r - a2i * a1i,
            a2r * a1i + a2i * a1r,
            a2r * b1r - a2i * b1i + b2r,
            a2r * b1i + a2i * b1r + b2i)


def s5_mixer(h, w_in, log_dt, a_re, a_im, b_re, b_im, c_re, c_im, d_skip, w_glu):
    f32 = jnp.float32
    b, l, _ = h.shape
    u = (h @ w_in).astype(f32)
    dt = jnp.exp(log_dt.astype(f32))[:, None]
    ar, ai = a_re.astype(f32), a_im.astype(f32)
    mag = jnp.exp(ar * dt)
    ang = ai * dt
    abar_r, abar_i = mag * jnp.cos(ang), mag * jnp.sin(ang)
    den = ar * ar + ai * ai
    nr, ni = abar_r - 1.0, abar_i
    cr = (nr * ar + ni * ai) / den
    ci = (ni * ar - nr * ai) / den
    br, bi = b_re.astype(f32), b_im.astype(f32)
    bbar_r = cr[..., None] * br - ci[..., None] * bi
    bbar_i = cr[..., None] * bi + ci[..., None] * br
    cre, cim = c_re.astype(f32), c_im.astype(f32)

    uc = u.reshape(b, l // CHUNK, CHUNK, S5_GROUPS, S5_GROUP).transpose(1, 0, 2, 3, 4)

    def step(carry, u_chunk):
        hr0, hi0 = carry
        bu_r = jnp.einsum('gpm,bcgm->bcgp', bbar_r, u_chunk)
        bu_i = jnp.einsum('gpm,bcgm->bcgp', bbar_i, u_chunk)
        a_r = jnp.broadcast_to(abar_r, bu_r.shape)
        a_i = jnp.broadcast_to(abar_i, bu_i.shape)
        pr, pi, sr, si = lax.associative_scan(_complex_combine, (a_r, a_i, bu_r, bu_i), axis=1)
        hr = pr * hr0[:, None] - pi * hi0[:, None] + sr
        hi = pr * hi0[:, None] + pi * hr0[:, None] + si
        y = jnp.einsum('gmp,bcgp->bcgm', cre, hr) - jnp.einsum('gmp,bcgp->bcgm', cim, hi)
        return (hr[:, -1], hi[:, -1]), y

    init = (jnp.zeros((b, S5_GROUPS, S5_STATE), f32), jnp.zeros((b, S5_GROUPS, S5_STATE), f32))
    _, y = lax.scan(step, init, uc)
    y = y.transpose(1, 0, 2, 3, 4).reshape(b, l, S5_WIDTH) + d_skip.astype(f32) * u
    z = jax.nn.gelu(y).astype(h.dtype)
    val, gate = jnp.split(z @ w_glu, 2, axis=-1)
    return val * jax.nn.sigmoid(gate)


def _rotate(t, cos, sin):
    t1, t2 = jnp.split(t, 2, axis=-1)
    return jnp.concatenate([t1 * cos - t2 * sin, t1 * sin + t2 * cos], axis=-1)


def retention_mixer(h, w_in, gn_g, w_out):
    f32 = jnp.float32
    b, l, _ = h.shape
    n = l // CHUNK
    proj = h @ w_in
    q, k, v, g = jnp.split(proj, [RET_QD, 2 * RET_QD, 2 * RET_QD + RET_VD], axis=-1)
    q = q.astype(f32).reshape(b, l, RET_HEADS, RET_DK)
    k = k.astype(f32).reshape(b, l, RET_HEADS, RET_DK)
    v = v.astype(f32).reshape(b, l, RET_HEADS, RET_DV)
    pos = jnp.arange(l, dtype=f32)
    inv_freq = ROPE_BASE ** (-jnp.arange(0, RET_DK, 2, dtype=f32) / RET_DK)
    theta = pos[:, None] * inv_freq[None, :]
    cos, sin = jnp.cos(theta)[:, None, :], jnp.sin(theta)[:, None, :]
    q = _rotate(q, cos, sin)
    k = _rotate(k, cos, sin) * (RET_DK ** -0.5)

    log_g = jnp.log(1.0 - 2.0 ** (-5.0 - jnp.arange(RET_HEADS, dtype=f32)))
    idx = jnp.arange(CHUNK, dtype=f32)
    diff = idx[:, None] - idx[None, :]
    dmask = jnp.where(diff[None] >= 0, jnp.exp(jnp.maximum(diff, 0.0)[None] * log_g[:, None, None]), 0.0)
    xi = jnp.exp((idx[None, :] + 1.0) * log_g[:, None])[..., None]
    zeta = jnp.exp((CHUNK - 1.0 - idx[None, :]) * log_g[:, None])[..., None]
    chunk_decay = jnp.exp(CHUNK * log_g)[:, None, None]

    def to_chunks(t):
        return t.reshape(b, n, CHUNK, RET_HEADS, t.shape[-1]).transpose(1, 0, 3, 2, 4)

    def step(R, qkv):
        qc, kc, vc = qkv
        s = jnp.einsum('bhtd,bhsd->bhts', qc, kc) * dmask
        inner = jnp.einsum('bhts,bhsv->bhtv', s, vc)
        cross = jnp.einsum('bhtd,bhdv->bhtv', qc, R) * xi
        R_new = chunk_decay * R + jnp.einsum('bhsd,bhsv->bhdv', kc * zeta, vc)
        return R_new, inner + cross

    R0 = jnp.zeros((b, RET_HEADS, RET_DK, RET_DV), f32)
    _, o = lax.scan(step, R0, (to_chunks(q), to_chunks(k), to_chunks(v)))
    o = o.transpose(1, 0, 3, 2, 4).reshape(b, l, RET_HEADS, RET_DV)
    mu = jnp.mean(o, axis=-1, keepdims=True)
    var = jnp.mean(jnp.square(o - mu), axis=-1, keepdims=True)
    o = ((o - mu) * lax.rsqrt(var + LN_EPS)).reshape(b, l, RET_VD) * gn_g.astype(f32)
    return (jax.nn.silu(g) * o.astype(h.dtype)) @ w_out


def shortconv_mixer(h, w_in, conv_w, w_out):
    gate_b, gate_c, xin = jnp.split(h @ w_in, 3, axis=-1)
    z = gate_c * xin
    y = lax.conv_general_dilated(z, conv_w, window_strides=(1,), padding=[(CONV_WIDTH - 1, 0)],
                                 dimension_numbers=('NWC', 'WIO', 'NWC'),
                                 feature_group_count=SC_WIDTH)
    return (gate_b * y) @ w_out


def channel_mlp(h, w1, w2):
    return jnp.square(jax.nn.relu(h @ w1)) @ w2


def setup_inputs(seed: int = 0) -> dict:
    key = jax.random.key(seed)
    ks = iter(jax.random.split(key, 40))
    f32 = jnp.float32
    nA, nB, nC, nD = _n_uses(0), _n_uses(1), _n_uses(2), _n_uses(3)

    def nrm(shape, std):
        return jax.random.normal(next(ks), shape, f32) * std

    def gain(shape):
        return 1.0 + 0.02 * jax.random.normal(next(ks), shape, f32)

    x = jax.random.normal(next(ks), (BATCH, SEQ, D_MODEL), f32)
    norm1_g = gain((DEPTH, D_MODEL))
    norm2_g = gain((DEPTH, D_MODEL))
    mlp_w1 = nrm((DEPTH, D_MODEL, D_FF), D_MODEL ** -0.5)
    mlp_w2 = nrm((DEPTH, D_FF, D_MODEL), D_FF ** -0.5)
    final_g = gain((D_MODEL,))

    a_w_in = nrm((nA, D_MODEL, 2 * GM_WIDTH), D_MODEL ** -0.5)
    a_ln_g = gain((nA, GM_WIDTH))
    a_ws = nrm((nA, GM_GROUPS, CHUNK, CHUNK), CHUNK ** -0.5)
    a_bs = gain((nA, GM_GROUPS, CHUNK))
    a_w_out = nrm((nA, GM_WIDTH, D_MODEL), GM_WIDTH ** -0.5)

    b_w_in = nrm((nB, D_MODEL, S5_WIDTH), D_MODEL ** -0.5)
    b_log_dt = jax.random.uniform(next(ks), (nB, S5_GROUPS), f32,
                                  math.log(S5_DT_MIN), math.log(S5_DT_MAX))
    b_a_re = -0.5 + 0.01 * jax.random.normal(next(ks), (nB, S5_GROUPS, S5_STATE), f32)
    b_a_im = (math.pi * jnp.arange(S5_STATE, dtype=f32))[None, None, :] \
        + 0.01 * jax.random.normal(next(ks), (nB, S5_GROUPS, S5_STATE), f32)
    b_b_re = nrm((nB, S5_GROUPS, S5_STATE, S5_GROUP), (2 * S5_GROUP) ** -0.5)
    b_b_im = nrm((nB, S5_GROUPS, S5_STATE, S5_GROUP), (2 * S5_GROUP) ** -0.5)
    b_c_re = nrm((nB, S5_GROUPS, S5_GROUP, S5_STATE), (2 * S5_STATE) ** -0.5)
    b_c_im = nrm((nB, S5_GROUPS, S5_GROUP, S5_STATE), (2 * S5_STATE) ** -0.5)
    b_d = nrm((nB, S5_WIDTH), 1.0)
    b_w_glu = nrm((nB, S5_WIDTH, 2 * D_MODEL), S5_WIDTH ** -0.5)

    c_w_in = nrm((nC, D_MODEL, 2 * RET_QD + 2 * RET_VD), D_MODEL ** -0.5)
    c_gn_g = gain((nC, RET_VD))
    c_w_out = nrm((nC, RET_VD, D_MODEL), RET_VD ** -0.5)

    d_w_in = nrm((nD, D_MODEL, 3 * SC_WIDTH), D_MODEL ** -0.5)
    d_conv_w = nrm((nD, CONV_WIDTH, 1, SC_WIDTH), CONV_WIDTH ** -0.5)
    d_w_out = nrm((nD, SC_WIDTH, D_MODEL), SC_WIDTH ** -0.5)

    return {"x": x, "norm1_g": norm1_g, "norm2_g": norm2_g, "mlp_w1": mlp_w1, "mlp_w2": mlp_w2,
            "final_g": final_g,
            "a_w_in": a_w_in, "a_ln_g": a_ln_g, "a_ws": a_ws, "a_bs": a_bs, "a_w_out": a_w_out,
            "b_w_in": b_w_in, "b_log_dt": b_log_dt, "b_a_re": b_a_re, "b_a_im": b_a_im,
            "b_b_re": b_b_re, "b_b_im": b_b_im, "b_c_re": b_c_re, "b_c_im": b_c_im,
            "b_d": b_d, "b_w_glu": b_w_glu,
            "c_w_in": c_w_in, "c_gn_g": c_gn_g, "c_w_out": c_w_out,
            "d_w_in": d_w_in, "d_conv_w": d_conv_w, "d_w_out": d_w_out}


def reference(x, norm1_g, norm2_g, mlp_w1, mlp_w2, final_g,
              a_w_in, a_ln_g, a_ws, a_bs, a_w_out,
              b_w_in, b_log_dt, b_a_re, b_a_im, b_b_re, b_b_im, b_c_re, b_c_im, b_d, b_w_glu,
              c_w_in, c_gn_g, c_w_out,
              d_w_in, d_conv_w, d_w_out):
    for i in range(DEPTH):
        m, j = i % N_MIXERS, i // N_MIXERS
        hn = rms_norm(x, norm1_g[i])
        if m == 0:
            y = gmlp_mixer(hn, a_w_in[j], a_ln_g[j], a_ws[j], a_bs[j], a_w_out[j])
        elif m == 1:
            y = s5_mixer(hn, b_w_in[j], b_log_dt[j], b_a_re[j], b_a_im[j], b_b_re[j], b_b_im[j],
                         b_c_re[j], b_c_im[j], b_d[j], b_w_glu[j])
        elif m == 2:
            y = retention_mixer(hn, c_w_in[j], c_gn_g[j], c_w_out[j])
        else:
            y = shortconv_mixer(hn, d_w_in[j], d_conv_w[j], d_w_out[j])
        x = x + y.astype(x.dtype)
        x = x + channel_mlp(rms_norm(x, norm2_g[i]), mlp_w1[i], mlp_w2[i]).astype(x.dtype)
    return rms_norm(x, final_g)
```

```python
import functools
import math

import jax
import jax.numpy as jnp
from jax import lax
from jax.experimental import pallas as pl
from jax.experimental.pallas import tpu as pltpu

F32 = jnp.float32
BF16 = jnp.bfloat16

EPS = 1e-6
LN_EPS = 1e-5
CHUNK = 128
GM_GROUPS = 8
S5_GROUP = 16
S5_STATE = 64
S5_COLBLOCK_GROUPS = 16
RET_HEADS = 4
ROPE_BASE = 10000.0
CONV_WIDTH = 3

LANES = 128
SUBLANES = 8
VMEM_LIMIT = 56 * 1024 * 1024


def _dot(a, b):
    return jnp.dot(a, b, preferred_element_type=F32)


def _rms(x, g):
    return x * lax.rsqrt(jnp.mean(x * x, axis=-1, keepdims=True) + EPS) * g


def _gelu(x):
    c = math.sqrt(2.0 / math.pi)
    return x * (0.5 * (1.0 + jnp.tanh(c * (x + 0.044715 * (x * x * x)))))


def _sigmoid(x):
    return 1.0 / (1.0 + jnp.exp(-x))


def _resident(shape):
    nd = len(shape)
    return pl.BlockSpec(shape, lambda *_: (0,) * nd, pipeline_mode=pl.Buffered(1))


def _params(semantics):
    return pltpu.CompilerParams(dimension_semantics=semantics, vmem_limit_bytes=VMEM_LIMIT)


def _mlp_kernel(x_ref, g_ref, w1_ref, w2_ref, fg_ref, o_ref, hn_ref, *, final_norm):
    j = pl.program_id(1)

    @pl.when(j == 0)
    def _():
        x = x_ref[...]
        hn_ref[...] = _rms(x, g_ref[...]).astype(BF16)
        o_ref[...] = x

    a = jnp.maximum(_dot(hn_ref[...], w1_ref[...]), 0.0)
    o_ref[...] += _dot((a * a).astype(BF16), w2_ref[...])

    if final_norm:
        @pl.when(j == pl.num_programs(1) - 1)
        def _():
            o_ref[...] = _rms(o_ref[...], fg_ref[...])


def _mlp(x2, g, w1, w2, final_g, *, final_norm, tm=512, tf=512):
    n, d = x2.shape
    dff = w1.shape[1]
    tm = min(tm, n)
    return pl.pallas_call(
        functools.partial(_mlp_kernel, final_norm=final_norm),
        grid=(n // tm, dff // tf),
        in_specs=[
            pl.BlockSpec((tm, d), lambda i, j: (i, 0)),
            pl.BlockSpec((1, d), lambda i, j: (0, 0)),
            pl.BlockSpec((d, tf), lambda i, j: (0, j)),
            pl.BlockSpec((tf, d), lambda i, j: (j, 0)),
            pl.BlockSpec((1, d), lambda i, j: (0, 0)),
        ],
        out_specs=pl.BlockSpec((tm, d), lambda i, j: (i, 0)),
        out_shape=jax.ShapeDtypeStruct((n, d), F32),
        scratch_shapes=[pltpu.VMEM((tm, d), BF16)],
        compiler_params=_params(("parallel", "arbitrary")),
        name="mlp_final" if final_norm else "mlp",
    )(x2, g, w1, w2, final_g)


def _gmlp_kernel(x_ref, g_ref, win_ref, lng_ref, ws_ref, bs_ref, wout_ref, o_ref,
                 u_scr, v_scr, vb_scr, gated_scr):
    tc = x_ref.shape[0]
    e = lng_ref.shape[1]
    n_groups = ws_ref.shape[0]
    gd = e // n_groups
    nb = 512
    x = x_ref[...]
    hn = _rms(x, g_ref[...]).astype(BF16)
    for c0 in range(0, e, nb):
        u_scr[:, c0:c0 + nb] = _gelu(_dot(hn, win_ref[:, c0:c0 + nb]))
        v_scr[:, c0:c0 + nb] = _gelu(_dot(hn, win_ref[:, e + c0:e + c0 + nb]))
    v = v_scr[...]
    mu = jnp.mean(v, axis=-1, keepdims=True)
    dv = v - mu
    var = jnp.mean(dv * dv, axis=-1, keepdims=True)
    vb_scr[...] = (dv * lax.rsqrt(var + LN_EPS) * lng_ref[...]).astype(BF16)
    row = lax.broadcasted_iota(jnp.int32, (CHUNK, CHUNK), 0)
    col = lax.broadcasted_iota(jnp.int32, (CHUNK, CHUNK), 1)
    for g in range(n_groups):
        ws = jnp.where(row >= col, ws_ref[g], 0.0).astype(BF16)
        cols = slice(g * gd, (g + 1) * gd)
        for r0 in range(0, tc, CHUNK):
            rows = slice(r0, r0 + CHUNK)
            sv = _dot(ws, vb_scr[rows, cols]) + bs_ref[g]
            gated_scr[rows, cols] = (u_scr[rows, cols] * sv).astype(BF16)
    o_ref[...] = x + _dot(gated_scr[...], wout_ref[...])


def _gmlp(x2, g, w_in, ln_g, ws, bs_b, w_out, *, tc=256):
    n, d = x2.shape
    e = ln_g.shape[1]
    tc = min(tc, n)
    return pl.pallas_call(
        _gmlp_kernel,
        grid=(n // tc,),
        in_specs=[
            pl.BlockSpec((tc, d), lambda i: (i, 0)),
            _resident((1, d)),
            _resident(w_in.shape),
            _resident((1, e)),
            _resident(ws.shape),
            _resident(bs_b.shape),
            _resident(w_out.shape),
        ],
        out_specs=pl.BlockSpec((tc, d), lambda i: (i, 0)),
        out_shape=jax.ShapeDtypeStruct((n, d), F32),
        scratch_shapes=[pltpu.VMEM((tc, e), F32), pltpu.VMEM((tc, e), F32),
                        pltpu.VMEM((tc, e), BF16), pltpu.VMEM((tc, e), BF16)],
        compiler_params=_params(("parallel",)),
        name="gmlp_mixer",
    )(x2, g, w_in, ln_g, ws, bs_b, w_out)


def _s5_pitch(tc):
    p = tc // SUBLANES
    return SUBLANES * (p + 1 if p % 2 == 0 else p + 2)


def _s5_kernel(x_ref, g_ref, win_ref, wb_ref, a_ref, wc_ref, d_ref, wglu_ref, o_ref,
               u_scr, s_scr, h_scr, z_scr, *, pitch):
    nbatch, tc, d = x_ref.shape
    e = d_ref.shape[1]
    ncb = wb_ref.shape[0]
    cw = S5_COLBLOCK_GROUPS * S5_GROUP
    ns = S5_COLBLOCK_GROUPS * S5_STATE
    nslab = ns // LANES

    @pl.when(pl.program_id(0) == 0)
    def _():
        h_scr[...] = jnp.zeros_like(h_scr)

    for b in range(nbatch):
        hn = _rms(x_ref[b], g_ref[...]).astype(BF16)
        u_scr[b] = _dot(hn, win_ref[...])
        for cb in range(ncb):
            bu = _dot(u_scr[b, :, cb * cw:(cb + 1) * cw].astype(BF16), wb_ref[cb])
            for part in range(2):
                idx = (b * ncb + cb) * 2 + part
                for k in range(nslab):
                    c0 = part * ns + k * LANES
                    s_scr[idx, k * pitch:k * pitch + tc, :] = bu[:, c0:c0 + LANES]

    ar = [a_ref[0, cb] for cb in range(ncb)]
    ai = [a_ref[1, cb] for cb in range(ncb)]
    nchain = nbatch * ncb

    def step(t, hs):
        out = []
        for c in range(nchain):
            cb = c % ncb
            hr, hi = hs[2 * c], hs[2 * c + 1]
            sl = pl.ds(t, SUBLANES, stride=pitch)
            nr = ar[cb] * hr - ai[cb] * hi + s_scr[2 * c, sl, :]
            ni = ar[cb] * hi + ai[cb] * hr + s_scr[2 * c + 1, sl, :]
            s_scr[2 * c, sl, :] = nr
            s_scr[2 * c + 1, sl, :] = ni
            out += [nr, ni]
        return tuple(out)

    hs = lax.fori_loop(0, tc, step, tuple(h_scr[i] for i in range(2 * nchain)))
    for i in range(2 * nchain):
        h_scr[i] = hs[i]

    for b in range(nbatch):
        for cb in range(ncb):
            hcat = jnp.concatenate(
                [s_scr[(b * ncb + cb) * 2 + part, k * pitch:k * pitch + tc, :].astype(BF16)
                 for part in range(2) for k in range(nslab)], axis=1)
            cols = slice(cb * cw, (cb + 1) * cw)
            y = _dot(hcat, wc_ref[cb]) + d_ref[:, cols] * u_scr[b, :, cols]
            z_scr[b, :, cols] = _gelu(y).astype(BF16)
        vg = _dot(z_scr[b], wglu_ref[...])
        o_ref[b] = x_ref[b] + vg[:, :d] * _sigmoid(vg[:, d:])


def _s5(x, g, w_in, wb, a_tiles, wc, d_skip, w_glu, *, tc=128):
    nbatch, l, d = x.shape
    e = d_skip.shape[1]
    ncb = wb.shape[0]
    tc = min(tc, l)
    pitch = _s5_pitch(tc)
    nslab = S5_COLBLOCK_GROUPS * S5_STATE // LANES
    return pl.pallas_call(
        functools.partial(_s5_kernel, pitch=pitch),
        grid=(l // tc,),
        in_specs=[
            pl.BlockSpec((nbatch, tc, d), lambda i: (0, i, 0)),
            _resident((1, d)),
            _resident(w_in.shape),
            _resident(wb.shape),
            _resident(a_tiles.shape),
            _resident(wc.shape),
            _resident((1, e)),
            _resident(w_glu.shape),
        ],
        out_specs=pl.BlockSpec((nbatch, tc, d), lambda i: (0, i, 0)),
        out_shape=jax.ShapeDtypeStruct((nbatch, l, d), F32),
        scratch_shapes=[
            pltpu.VMEM((nbatch, tc, e), F32),
            pltpu.VMEM((nbatch * ncb * 2, nslab * pitch, LANES), F32),
            pltpu.VMEM((nbatch * ncb * 2, SUBLANES, LANES), F32),
            pltpu.VMEM((nbatch, tc, e), BF16),
        ],
        compiler_params=_params(("arbitrary",)),
        name="s5_mixer",
    )(x, g, w_in, wb, a_tiles, wc, d_skip, w_glu)


def _s5_prepare(log_dt, a_re, a_im, b_re, b_im, c_re, c_im):
    ngroups, nstate = a_re.shape
    m = b_re.shape[2]
    dt = jnp.exp(log_dt.astype(F32))[:, None]
    ar, ai = a_re.astype(F32), a_im.astype(F32)
    mag = jnp.exp(ar * dt)
    ang = ai * dt
    abar_r, abar_i = mag * jnp.cos(ang), mag * jnp.sin(ang)
    den = ar * ar + ai * ai
    nr, ni = abar_r - 1.0, abar_i
    cr = (nr * ar + ni * ai) / den
    ci = (ni * ar - nr * ai) / den
    br, bi = b_re.astype(F32), b_im.astype(F32)
    bbar_r = cr[..., None] * br - ci[..., None] * bi
    bbar_i = cr[..., None] * bi + ci[..., None] * br
    gpb = S5_COLBLOCK_GROUPS
    ncb = ngroups // gpb
    eye = jnp.eye(gpb, dtype=F32)

    def blockdiag_in(w):
        w = w.reshape(ncb, gpb, nstate, m)
        return jnp.einsum('cgpm,gh->cgmhp', w, eye).reshape(ncb, gpb * m, gpb * nstate)

    def blockdiag_out(w):
        w = w.reshape(ncb, gpb, m, nstate)
        return jnp.einsum('cgmp,gh->cgphm', w, eye).reshape(ncb, gpb * nstate, gpb * m)

    wb = jnp.concatenate([blockdiag_in(bbar_r), blockdiag_in(bbar_i)], axis=2).astype(BF16)
    wc = jnp.concatenate([blockdiag_out(c_re.astype(F32)),
                          blockdiag_out(-c_im.astype(F32))], axis=1).astype(BF16)
    a_tiles = jnp.stack([abar_r, abar_i]).reshape(2, ncb, SUBLANES, LANES)
    return wb, a_tiles, wc


def _ret_kernel(x_ref, g_ref, win_ref, cos_ref, sin_ref, gng_ref, wout_ref, o_ref,
                r_scr, gated_scr):
    nbatch, tc, d = x_ref.shape
    nheads = RET_HEADS
    dk = r_scr.shape[1]
    dv = r_scr.shape[2]
    qd = nheads * dk
    vd = nheads * dv
    half = dk // 2

    @pl.when(pl.program_id(0) == 0)
    def _():
        r_scr[...] = jnp.zeros_like(r_scr)

    cos = cos_ref[...]
    sin = sin_ref[...]
    row = lax.broadcasted_iota(jnp.int32, (tc, tc), 0)
    col = lax.broadcasted_iota(jnp.int32, (tc, tc), 1)
    diff = (row - col).astype(F32)
    t_v = lax.broadcasted_iota(jnp.int32, (tc, dv), 0).astype(F32)
    t_k = lax.broadcasted_iota(jnp.int32, (tc, dk), 0).astype(F32)

    def rotate(t):
        t1, t2 = t[:, :half], t[:, half:]
        return jnp.concatenate([t1 * cos - t2 * sin, t1 * sin + t2 * cos], axis=1)

    for b in range(nbatch):
        x = x_ref[b]
        hn = _rms(x, g_ref[...]).astype(BF16)
        for h in range(nheads):
            log_g = math.log(1.0 - 2.0 ** (-5.0 - h))
            dmask = jnp.where(diff >= 0, jnp.exp(jnp.maximum(diff, 0.0) * log_g), 0.0)
            xi = jnp.exp((t_v + 1.0) * log_g)
            zeta = jnp.exp((tc - 1.0 - t_k) * log_g)
            q = rotate(_dot(hn, win_ref[:, h * dk:(h + 1) * dk]))
            k = rotate(_dot(hn, win_ref[:, qd + h * dk:qd + (h + 1) * dk])) * (dk ** -0.5)
            v = _dot(hn, win_ref[:, 2 * qd + h * dv:2 * qd + (h + 1) * dv]).astype(BF16)
            gate = _dot(hn, win_ref[:, 2 * qd + vd + h * dv:2 * qd + vd + (h + 1) * dv])
            qb = q.astype(BF16)
            s = lax.dot_general(qb, k.astype(BF16), (((1,), (1,)), ((), ())),
                                preferred_element_type=F32) * dmask
            r_old = r_scr[b * nheads + h]
            o = _dot(s.astype(BF16), v) + _dot(qb, r_old.astype(BF16)) * xi
            r_scr[b * nheads + h] = math.exp(tc * log_g) * r_old + lax.dot_general(
                (k * zeta).astype(BF16), v, (((0,), (0,)), ((), ())), preferred_element_type=F32)
            mu = jnp.mean(o, axis=-1, keepdims=True)
            do = o - mu
            var = jnp.mean(do * do, axis=-1, keepdims=True)
            on = do * lax.rsqrt(var + LN_EPS) * gng_ref[:, h * dv:(h + 1) * dv]
            gated_scr[:, h * dv:(h + 1) * dv] = (gate * _sigmoid(gate) * on).astype(BF16)
        o_ref[b] = x + _dot(gated_scr[...], wout_ref[...])


def _retention(x, g, w_in, cos, sin, gn_g, w_out):
    nbatch, l, d = x.shape
    tc = CHUNK
    dk = d // RET_HEADS
    dv = 2 * dk
    vd = RET_HEADS * dv
    return pl.pallas_call(
        _ret_kernel,
        grid=(l // tc,),
        in_specs=[
            pl.BlockSpec((nbatch, tc, d), lambda i: (0, i, 0)),
            _resident((1, d)),
            _resident(w_in.shape),
            pl.BlockSpec((tc, dk // 2), lambda i: (i, 0)),
            pl.BlockSpec((tc, dk // 2), lambda i: (i, 0)),
            _resident((1, vd)),
            _resident(w_out.shape),
        ],
        out_specs=pl.BlockSpec((nbatch, tc, d), lambda i: (0, i, 0)),
        out_shape=jax.ShapeDtypeStruct((nbatch, l, d), F32),
        scratch_shapes=[pltpu.VMEM((nbatch * RET_HEADS, dk, dv), F32),
                        pltpu.VMEM((tc, vd), BF16)],
        compiler_params=_params(("arbitrary",)),
        name="retention_mixer",
    )(x, g, w_in, cos, sin, gn_g, w_out)


def _sconv_kernel(x_ref, g_ref, win_ref, cw_ref, wout_ref, o_ref, zs_scr):
    nbatch, tc, d = x_ref.shape
    e = cw_ref.shape[1]
    pad = SUBLANES

    @pl.when(pl.program_id(0) == 0)
    def _():
        zs_scr[:, 0:pad, :] = jnp.zeros((nbatch, pad, e), F32)

    for b in range(nbatch):
        x = x_ref[b]
        hn = _rms(x, g_ref[...]).astype(BF16)
        gate_b = _dot(hn, win_ref[:, 0:e])
        z = _dot(hn, win_ref[:, e:2 * e]) * _dot(hn, win_ref[:, 2 * e:3 * e])
        zs_scr[b, pad:pad + tc, :] = z
        y = (cw_ref[0:1, :] * zs_scr[b, pad - 2:pad - 2 + tc, :]
             + cw_ref[1:2, :] * zs_scr[b, pad - 1:pad - 1 + tc, :]
             + cw_ref[2:3, :] * z)
        zs_scr[b, 0:pad, :] = zs_scr[b, tc:tc + pad, :]
        o_ref[b] = x + _dot((gate_b * y).astype(BF16), wout_ref[...])


def _sconv(x, g, w_in, conv_w, w_out, *, tc=256):
    nbatch, l, d = x.shape
    e = conv_w.shape[1]
    tc = min(tc, l)
    return pl.pallas_call(
        _sconv_kernel,
        grid=(l // tc,),
        in_specs=[
            pl.BlockSpec((nbatch, tc, d), lambda i: (0, i, 0)),
            _resident((1, d)),
            _resident(w_in.shape),
            _resident(conv_w.shape),
            _resident(w_out.shape),
        ],
        out_specs=pl.BlockSpec((nbatch, tc, d), lambda i: (0, i, 0)),
        out_shape=jax.ShapeDtypeStruct((nbatch, l, d), F32),
        scratch_shapes=[pltpu.VMEM((nbatch, tc + SUBLANES, e), F32)],
        compiler_params=_params(("arbitrary",)),
        name="sconv_mixer",
    )(x, g, w_in, conv_w, w_out)


def kernel(x, norm1_g, norm2_g, mlp_w1, mlp_w2, final_g, a_w_in, a_ln_g, a_ws, a_bs, a_w_out, b_w_in, b_log_dt, b_a_re, b_a_im, b_b_re, b_b_im, b_c_re, b_c_im, b_d, b_w_glu, c_w_in, c_gn_g, c_w_out, d_w_in, d_conv_w, d_w_out):
    nbatch, l, d = x.shape
    depth = norm1_g.shape[0]
    n_mixers = 4
    bf = lambda w: w.astype(BF16)
    row = lambda v: v.reshape(1, -1).astype(F32)

    pos = jnp.arange(l, dtype=F32)
    dk = d // RET_HEADS
    inv_freq = ROPE_BASE ** (-jnp.arange(0, dk, 2, dtype=F32) / dk)
    theta = pos[:, None] * inv_freq[None, :]
    cos, sin = jnp.cos(theta), jnp.sin(theta)

    for i in range(depth):
        m, j = i % n_mixers, i // n_mixers
        g1 = row(norm1_g[i])
        if m == 0:
            gd = a_ln_g.shape[1] // GM_GROUPS
            bs_b = jnp.broadcast_to(a_bs[j].astype(F32)[:, :, None], (GM_GROUPS, CHUNK, gd))
            x = _gmlp(x.reshape(nbatch * l, d), g1, bf(a_w_in[j]), row(a_ln_g[j]),
                      a_ws[j].astype(F32), bs_b, bf(a_w_out[j])).reshape(nbatch, l, d)
        elif m == 1:
            wb, a_tiles, wc = _s5_prepare(b_log_dt[j], b_a_re[j], b_a_im[j], b_b_re[j],
                                          b_b_im[j], b_c_re[j], b_c_im[j])
            x = _s5(x, g1, bf(b_w_in[j]), wb, a_tiles, wc, row(b_d[j]), bf(b_w_glu[j]))
        elif m == 2:
            x = _retention(x, g1, bf(c_w_in[j]), cos, sin, row(c_gn_g[j]), bf(c_w_out[j]))
        else:
            x = _sconv(x, g1, bf(d_w_in[j]), d_conv_w[j].reshape(CONV_WIDTH, -1).astype(F32),
                       bf(d_w_out[j]))
        x = _mlp(x.reshape(nbatch * l, d), row(norm2_g[i]), bf(mlp_w1[i]), bf(mlp_w2[i]),
                 row(final_g), final_norm=(i == depth - 1)).reshape(nbatch, l, d)
    return x
```

```python
import functools
import math

import jax
import jax.numpy as jnp
from jax import lax
from jax.experimental import pallas as pl
from jax.experimental.pallas import tpu as pltpu

F32 = jnp.float32
BF16 = jnp.bfloat16

EPS = 1e-6
LN_EPS = 1e-5
CHUNK = 128
GM_GROUPS = 8
S5_GROUP = 16
S5_STATE = 64
S5_COLBLOCK_GROUPS = 16
RET_HEADS = 4
ROPE_BASE = 10000.0
CONV_WIDTH = 3

LANES = 128
SUBLANES = 8
VMEM_LIMIT = 56 * 1024 * 1024


def _dot(a, b):
    return jnp.dot(a, b, preferred_element_type=F32)


def _rms(x, g):
    return x * lax.rsqrt(jnp.mean(x * x, axis=-1, keepdims=True) + EPS) * g


def _gelu(x):
    c = math.sqrt(2.0 / math.pi)
    return x * (0.5 * (1.0 + jnp.tanh(c * (x + 0.044715 * (x * x * x)))))


def _sigmoid(x):
    return 1.0 / (1.0 + jnp.exp(-x))


def _resident(shape):
    nd = len(shape)
    return pl.BlockSpec(shape, lambda *_: (0,) * nd, pipeline_mode=pl.Buffered(1))


def _params(semantics):
    return pltpu.CompilerParams(dimension_semantics=semantics, vmem_limit_bytes=VMEM_LIMIT)


def _mlp_kernel(x_ref, g_ref, w1_ref, w2_ref, fg_ref, o_ref, hn_scr, a_scr, *, final_norm, nf):
    dff = w1_ref.shape[1]
    hn_scr[...] = _rms(x_ref[...], g_ref[...]).astype(BF16)
    for n0 in range(0, dff, nf):
        a = jnp.maximum(_dot(hn_scr[...], w1_ref[:, n0:n0 + nf]), 0.0)
        a_scr[:, n0:n0 + nf] = (a * a).astype(BF16)
    y = x_ref[...] + _dot(a_scr[...], w2_ref[...])
    if final_norm:
        y = _rms(y, fg_ref[...])
    o_ref[...] = y


def _mlp(x2, g, w1, w2, final_g, *, final_norm, tm=512, nf=512):
    n, d = x2.shape
    dff = w1.shape[1]
    tm = min(tm, n)
    return pl.pallas_call(
        functools.partial(_mlp_kernel, final_norm=final_norm, nf=nf),
        grid=(n // tm,),
        in_specs=[
            pl.BlockSpec((tm, d), lambda i: (i, 0)),
            _resident((1, d)),
            _resident(w1.shape),
            _resident(w2.shape),
            _resident((1, d)),
        ],
        out_specs=pl.BlockSpec((tm, d), lambda i: (i, 0)),
        out_shape=jax.ShapeDtypeStruct((n, d), F32),
        scratch_shapes=[pltpu.VMEM((tm, d), BF16), pltpu.VMEM((tm, dff), BF16)],
        compiler_params=_params(("parallel",)),
        name="mlp_final" if final_norm else "mlp",
    )(x2, g, w1, w2, final_g)


def _gmlp_kernel(x_ref, g_ref, win_ref, lng_ref, ws_ref, bs_ref, wout_ref, o_ref,
                 hn_scr, u_scr, v_scr, vb_scr, gated_scr):
    tc = x_ref.shape[0]
    e = lng_ref.shape[1]
    n_groups = ws_ref.shape[0]
    gd = e // n_groups
    nb = 512
    hn_scr[...] = _rms(x_ref[...], g_ref[...]).astype(BF16)
    for c0 in range(0, e, nb):
        u_scr[:, c0:c0 + nb] = _gelu(_dot(hn_scr[...], win_ref[:, c0:c0 + nb]))
        v_scr[:, c0:c0 + nb] = _gelu(_dot(hn_scr[...], win_ref[:, e + c0:e + c0 + nb]))
    v = v_scr[...]
    mu = jnp.mean(v, axis=-1, keepdims=True)
    dv = v - mu
    var = jnp.mean(dv * dv, axis=-1, keepdims=True)
    vb_scr[...] = (dv * lax.rsqrt(var + LN_EPS) * lng_ref[...]).astype(BF16)
    row = lax.broadcasted_iota(jnp.int32, (CHUNK, CHUNK), 0)
    col = lax.broadcasted_iota(jnp.int32, (CHUNK, CHUNK), 1)
    for g in range(n_groups):
        ws = jnp.where(row >= col, ws_ref[g], 0.0).astype(BF16)
        cols = slice(g * gd, (g + 1) * gd)
        for r0 in range(0, tc, CHUNK):
            rows = slice(r0, r0 + CHUNK)
            sv = _dot(ws, vb_scr[rows, cols]) + bs_ref[g]
            gated_scr[rows, cols] = (u_scr[rows, cols] * sv).astype(BF16)
    o_ref[...] = x_ref[...] + _dot(gated_scr[...], wout_ref[...])


def _gmlp(x2, g, w_in, ln_g, ws, bs_b, w_out, *, tc=512):
    n, d = x2.shape
    e = ln_g.shape[1]
    tc = min(tc, n)
    return pl.pallas_call(
        _gmlp_kernel,
        grid=(n // tc,),
        in_specs=[
            pl.BlockSpec((tc, d), lambda i: (i, 0)),
            _resident((1, d)),
            _resident(w_in.shape),
            _resident((1, e)),
            _resident(ws.shape),
            _resident(bs_b.shape),
            _resident(w_out.shape),
        ],
        out_specs=pl.BlockSpec((tc, d), lambda i: (i, 0)),
        out_shape=jax.ShapeDtypeStruct((n, d), F32),
        scratch_shapes=[pltpu.VMEM((tc, d), BF16),
                        pltpu.VMEM((tc, e), F32), pltpu.VMEM((tc, e), F32),
                        pltpu.VMEM((tc, e), BF16), pltpu.VMEM((tc, e), BF16)],
        compiler_params=_params(("parallel",)),
        name="gmlp_mixer",
    )(x2, g, w_in, ln_g, ws, bs_b, w_out)


def _s5_pitch(tc):
    p = tc // SUBLANES
    return SUBLANES * (p + 1 if p % 2 == 0 else p + 2)


def _s5_kernel(x_ref, g_ref, win_ref, wb_ref, a_ref, wc_ref, d_ref, wglu_ref, o_ref,
               hn_scr, u_scr, s_scr, h_scr, hc_scr, z_scr, *, pitch):
    nbatch, tc, d = x_ref.shape
    e = d_ref.shape[1]
    ncb = wb_ref.shape[0]
    cw = S5_COLBLOCK_GROUPS * S5_GROUP
    ns = S5_COLBLOCK_GROUPS * S5_STATE
    nslab = ns // LANES

    @pl.when(pl.program_id(0) == 0)
    def _():
        h_scr[...] = jnp.zeros_like(h_scr)

    nw = 2 * LANES
    for b in range(nbatch):
        hn_scr[b * tc:(b + 1) * tc, :] = _rms(x_ref[b], g_ref[...]).astype(BF16)
    for c0 in range(0, e, 2 * nw):
        u_scr[:, c0:c0 + 2 * nw] = _dot(hn_scr[...], win_ref[:, c0:c0 + 2 * nw])
    for cb in range(ncb):
        ub = u_scr[:, cb * cw:(cb + 1) * cw].astype(BF16)
        for c0 in range(0, 2 * ns, nw):
            bu = _dot(ub, wb_ref[cb, :, c0:c0 + nw])
            part, k0 = c0 // ns, (c0 % ns) // LANES
            for b in range(nbatch):
                idx = (b * ncb + cb) * 2 + part
                for kk in range(nw // LANES):
                    r0 = (k0 + kk) * pitch
                    s_scr[idx, r0:r0 + tc, :] = bu[b * tc:(b + 1) * tc, kk * LANES:(kk + 1) * LANES]

    ar = [a_ref[0, cb] for cb in range(ncb)]
    ai = [a_ref[1, cb] for cb in range(ncb)]
    nchain = nbatch * ncb

    def step(t, hs):
        out = []
        for c in range(nchain):
            cb = c % ncb
            hr, hi = hs[2 * c], hs[2 * c + 1]
            sl = pl.ds(t, SUBLANES, stride=pitch)
            nr = ar[cb] * hr - ai[cb] * hi + s_scr[2 * c, sl, :]
            ni = ar[cb] * hi + ai[cb] * hr + s_scr[2 * c + 1, sl, :]
            s_scr[2 * c, sl, :] = nr
            s_scr[2 * c + 1, sl, :] = ni
            out += [nr, ni]
        return tuple(out)

    hs = lax.fori_loop(0, tc, step, tuple(h_scr[i] for i in range(2 * nchain)), unroll=4)
    for i in range(2 * nchain):
        h_scr[i] = hs[i]

    for cb in range(ncb):
        for b in range(nbatch):
            for part in range(2):
                idx = (b * ncb + cb) * 2 + part
                for k in range(nslab):
                    c0 = part * ns + k * LANES
                    hc_scr[b * tc:(b + 1) * tc, c0:c0 + LANES] = (
                        s_scr[idx, k * pitch:k * pitch + tc, :].astype(BF16))
        cols = slice(cb * cw, (cb + 1) * cw)
        y = _dot(hc_scr[...], wc_ref[cb]) + d_ref[:, cols] * u_scr[:, cols]
        z_scr[:, cols] = _gelu(y).astype(BF16)
    for c0 in range(0, d, 2 * nw):
        val = _dot(z_scr[...], wglu_ref[:, c0:c0 + 2 * nw])
        gate = _dot(z_scr[...], wglu_ref[:, d + c0:d + c0 + 2 * nw])
        res = val * _sigmoid(gate)
        for b in range(nbatch):
            o_ref[b, :, c0:c0 + 2 * nw] = x_ref[b, :, c0:c0 + 2 * nw] + res[b * tc:(b + 1) * tc]


def _s5(x, g, w_in, wb, a_tiles, wc, d_skip, w_glu, *, tc=256):
    nbatch, l, d = x.shape
    e = d_skip.shape[1]
    ncb = wb.shape[0]
    tc = min(tc, l)
    pitch = _s5_pitch(tc)
    nslab = S5_COLBLOCK_GROUPS * S5_STATE // LANES
    return pl.pallas_call(
        functools.partial(_s5_kernel, pitch=pitch),
        grid=(l // tc,),
        in_specs=[
            pl.BlockSpec((nbatch, tc, d), lambda i: (0, i, 0)),
            _resident((1, d)),
            _resident(w_in.shape),
            _resident(wb.shape),
            _resident(a_tiles.shape),
            _resident(wc.shape),
            _resident((1, e)),
            _resident(w_glu.shape),
        ],
        out_specs=pl.BlockSpec((nbatch, tc, d), lambda i: (0, i, 0)),
        out_shape=jax.ShapeDtypeStruct((nbatch, l, d), F32),
        scratch_shapes=[
            pltpu.VMEM((nbatch * tc, d), BF16),
            pltpu.VMEM((nbatch * tc, e), F32),
            pltpu.VMEM((nbatch * ncb * 2, nslab * pitch, LANES), F32),
            pltpu.VMEM((nbatch * ncb * 2, SUBLANES, LANES), F32),
            pltpu.VMEM((nbatch * tc, 2 * nslab * LANES), BF16),
            pltpu.VMEM((nbatch * tc, e), BF16),
        ],
        compiler_params=_params(("arbitrary",)),
        name="s5_mixer",
    )(x, g, w_in, wb, a_tiles, wc, d_skip, w_glu)


def _s5_prepare(log_dt, a_re, a_im, b_re, b_im, c_re, c_im):
    ngroups, nstate = a_re.shape
    m = b_re.shape[2]
    dt = jnp.exp(log_dt.astype(F32))[:, None]
    ar, ai = a_re.astype(F32), a_im.astype(F32)
    mag = jnp.exp(ar * dt)
    ang = ai * dt
    abar_r, abar_i = mag * jnp.cos(ang), mag * jnp.sin(ang)
    den = ar * ar + ai * ai
    nr, ni = abar_r - 1.0, abar_i
    cr = (nr * ar + ni * ai) / den
    ci = (ni * ar - nr * ai) / den
    br, bi = b_re.astype(F32), b_im.astype(F32)
    bbar_r = cr[..., None] * br - ci[..., None] * bi
    bbar_i = cr[..., None] * bi + ci[..., None] * br
    gpb = S5_COLBLOCK_GROUPS
    ncb = ngroups // gpb
    eye = jnp.eye(gpb, dtype=F32)

    def blockdiag_in(w):
        w = w.reshape(ncb, gpb, nstate, m)
        return jnp.einsum('cgpm,gh->cgmhp', w, eye).reshape(ncb, gpb * m, gpb * nstate)

    def blockdiag_out(w):
        w = w.reshape(ncb, gpb, m, nstate)
        return jnp.einsum('cgmp,gh->cgphm', w, eye).reshape(ncb, gpb * nstate, gpb * m)

    wb = jnp.concatenate([blockdiag_in(bbar_r), blockdiag_in(bbar_i)], axis=2).astype(BF16)
    wc = jnp.concatenate([blockdiag_out(c_re.astype(F32)),
                          blockdiag_out(-c_im.astype(F32))], axis=1).astype(BF16)
    a_tiles = jnp.stack([abar_r, abar_i]).reshape(2, ncb, SUBLANES, LANES)
    return wb, a_tiles, wc


def _ret_kernel(x_ref, g_ref, win_ref, cos_ref, sin_ref, gng_ref, wout_ref, o_ref,
                hn_scr, r_scr, gated_scr):
    nbatch, tc, d = x_ref.shape
    nheads = RET_HEADS
    dk = r_scr.shape[1]
    dv = r_scr.shape[2]
    qd = nheads * dk
    vd = nheads * dv
    half = dk // 2

    @pl.when(pl.program_id(0) == 0)
    def _():
        r_scr[...] = jnp.zeros_like(r_scr)

    for b in range(nbatch):
        hn_scr[b * tc:(b + 1) * tc, :] = _rms(x_ref[b], g_ref[...]).astype(BF16)

    cos = jnp.concatenate([cos_ref[...]] * nbatch, axis=0)
    sin = jnp.concatenate([sin_ref[...]] * nbatch, axis=0)
    row = lax.broadcasted_iota(jnp.int32, (tc, tc), 0)
    col = lax.broadcasted_iota(jnp.int32, (tc, tc), 1)
    diff = (row - col).astype(F32)
    t_v = lax.broadcasted_iota(jnp.int32, (tc, dv), 0).astype(F32)
    t_k = lax.broadcasted_iota(jnp.int32, (tc, dk), 0).astype(F32)

    def rotate(t):
        t1, t2 = t[:, :half], t[:, half:]
        return jnp.concatenate([t1 * cos - t2 * sin, t1 * sin + t2 * cos], axis=1)

    for h in range(nheads):
        log_g = math.log(1.0 - 2.0 ** (-5.0 - h))
        dmask = jnp.where(diff >= 0, jnp.exp(jnp.maximum(diff, 0.0) * log_g), 0.0)
        xi = jnp.exp((t_v + 1.0) * log_g)
        zeta = jnp.exp((tc - 1.0 - t_k) * log_g)
        hn = hn_scr[...]
        q = rotate(_dot(hn, win_ref[:, h * dk:(h + 1) * dk])).astype(BF16)
        k = rotate(_dot(hn, win_ref[:, qd + h * dk:qd + (h + 1) * dk])) * (dk ** -0.5)
        v = _dot(hn, win_ref[:, 2 * qd + h * dv:2 * qd + (h + 1) * dv]).astype(BF16)
        gate = _dot(hn, win_ref[:, 2 * qd + vd + h * dv:2 * qd + vd + (h + 1) * dv])
        for b in range(nbatch):
            rows = slice(b * tc, (b + 1) * tc)
            kb = k[rows]
            s = lax.dot_general(q[rows], kb.astype(BF16), (((1,), (1,)), ((), ())),
                                preferred_element_type=F32) * dmask
            r_old = r_scr[b * nheads + h]
            o = _dot(s.astype(BF16), v[rows]) + _dot(q[rows], r_old.astype(BF16)) * xi
            r_scr[b * nheads + h] = math.exp(tc * log_g) * r_old + lax.dot_general(
                (kb * zeta).astype(BF16), v[rows], (((0,), (0,)), ((), ())),
                preferred_element_type=F32)
            mu = jnp.mean(o, axis=-1, keepdims=True)
            do = o - mu
            var = jnp.mean(do * do, axis=-1, keepdims=True)
            on = do * lax.rsqrt(var + LN_EPS) * gng_ref[:, h * dv:(h + 1) * dv]
            gb = gate[rows]
            gated_scr[rows, h * dv:(h + 1) * dv] = (gb * _sigmoid(gb) * on).astype(BF16)
    nw = 512
    for c0 in range(0, d, nw):
        res = _dot(gated_scr[...], wout_ref[:, c0:c0 + nw])
        for b in range(nbatch):
            o_ref[b, :, c0:c0 + nw] = x_ref[b, :, c0:c0 + nw] + res[b * tc:(b + 1) * tc]


def _retention(x, g, w_in, cos, sin, gn_g, w_out, *, tc=256):
    nbatch, l, d = x.shape
    tc = min(tc, l)
    dk = d // RET_HEADS
    dv = 2 * dk
    vd = RET_HEADS * dv
    return pl.pallas_call(
        _ret_kernel,
        grid=(l // tc,),
        in_specs=[
            pl.BlockSpec((nbatch, tc, d), lambda i: (0, i, 0)),
            _resident((1, d)),
            _resident(w_in.shape),
            pl.BlockSpec((tc, dk // 2), lambda i: (i, 0)),
            pl.BlockSpec((tc, dk // 2), lambda i: (i, 0)),
            _resident((1, vd)),
            _resident(w_out.shape),
        ],
        out_specs=pl.BlockSpec((nbatch, tc, d), lambda i: (0, i, 0)),
        out_shape=jax.ShapeDtypeStruct((nbatch, l, d), F32),
        scratch_shapes=[pltpu.VMEM((nbatch * tc, d), BF16),
                        pltpu.VMEM((nbatch * RET_HEADS, dk, dv), F32),
                        pltpu.VMEM((nbatch * tc, vd), BF16)],
        compiler_params=_params(("arbitrary",)),
        name="retention_mixer",
    )(x, g, w_in, cos, sin, gn_g, w_out)


def _sconv_kernel(x_ref, g_ref, win_ref, cw_ref, wout_ref, o_ref, hn_scr, zs_scr, gy_scr):
    nbatch, tc, d = x_ref.shape
    e = cw_ref.shape[1]
    pad = SUBLANES

    @pl.when(pl.program_id(0) == 0)
    def _():
        zs_scr[:, 0:pad, :] = jnp.zeros((nbatch, pad, e), F32)

    for b in range(nbatch):
        hn_scr[b * tc:(b + 1) * tc, :] = _rms(x_ref[b], g_ref[...]).astype(BF16)
    nw = 512
    for c0 in range(0, e, nw):
        cols = slice(c0, c0 + nw)
        hn = hn_scr[...]
        gate_b = _dot(hn, win_ref[:, c0:c0 + nw])
        z = _dot(hn, win_ref[:, e + c0:e + c0 + nw]) * _dot(hn, win_ref[:, 2 * e + c0:2 * e + c0 + nw])
        for b in range(nbatch):
            rows = slice(b * tc, (b + 1) * tc)
            zs_scr[b, pad:pad + tc, cols] = z[rows]
            y = (cw_ref[0:1, cols] * zs_scr[b, pad - 2:pad - 2 + tc, cols]
                 + cw_ref[1:2, cols] * zs_scr[b, pad - 1:pad - 1 + tc, cols]
                 + cw_ref[2:3, cols] * z[rows])
            zs_scr[b, 0:pad, cols] = zs_scr[b, tc:tc + pad, cols]
            gy_scr[rows, cols] = (gate_b[rows] * y).astype(BF16)
    for c0 in range(0, d, nw):
        res = _dot(gy_scr[...], wout_ref[:, c0:c0 + nw])
        for b in range(nbatch):
            o_ref[b, :, c0:c0 + nw] = x_ref[b, :, c0:c0 + nw] + res[b * tc:(b + 1) * tc]


def _sconv(x, g, w_in, conv_w, w_out, *, tc=256):
    nbatch, l, d = x.shape
    e = conv_w.shape[1]
    tc = min(tc, l)
    return pl.pallas_call(
        _sconv_kernel,
        grid=(l // tc,),
        in_specs=[
            pl.BlockSpec((nbatch, tc, d), lambda i: (0, i, 0)),
            _resident((1, d)),
            _resident(w_in.shape),
            _resident(conv_w.shape),
            _resident(w_out.shape),
        ],
        out_specs=pl.BlockSpec((nbatch, tc, d), lambda i: (0, i, 0)),
        out_shape=jax.ShapeDtypeStruct((nbatch, l, d), F32),
        scratch_shapes=[pltpu.VMEM((nbatch * tc, d), BF16),
                        pltpu.VMEM((nbatch, tc + SUBLANES, e), F32),
                        pltpu.VMEM((nbatch * tc, e), BF16)],
        compiler_params=_params(("arbitrary",)),
        name="sconv_mixer",
    )(x, g, w_in, conv_w, w_out)


def kernel(x, norm1_g, norm2_g, mlp_w1, mlp_w2, final_g, a_w_in, a_ln_g, a_ws, a_bs, a_w_out, b_w_in, b_log_dt, b_a_re, b_a_im, b_b_re, b_b_im, b_c_re, b_c_im, b_d, b_w_glu, c_w_in, c_gn_g, c_w_out, d_w_in, d_conv_w, d_w_out):
    nbatch, l, d = x.shape
    depth = norm1_g.shape[0]
    n_mixers = 4
    bf = lambda w: w.astype(BF16)
    row = lambda v: v.reshape(1, -1).astype(F32)

    pos = jnp.arange(l, dtype=F32)
    dk = d // RET_HEADS
    inv_freq = ROPE_BASE ** (-jnp.arange(0, dk, 2, dtype=F32) / dk)
    theta = pos[:, None] * inv_freq[None, :]
    cos, sin = jnp.cos(theta), jnp.sin(theta)

    for i in range(depth):
        m, j = i % n_mixers, i // n_mixers
        g1 = row(norm1_g[i])
        if m == 0:
            gd = a_ln_g.shape[1] // GM_GROUPS
            bs_b = jnp.broadcast_to(a_bs[j].astype(F32)[:, :, None], (GM_GROUPS, CHUNK, gd))
            x = _gmlp(x.reshape(nbatch * l, d), g1, bf(a_w_in[j]), row(a_ln_g[j]),
                      a_ws[j].astype(F32), bs_b, bf(a_w_out[j])).reshape(nbatch, l, d)
        elif m == 1:
            wb, a_tiles, wc = _s5_prepare(b_log_dt[j], b_a_re[j], b_a_im[j], b_b_re[j],
                                          b_b_im[j], b_c_re[j], b_c_im[j])
            x = _s5(x, g1, bf(b_w_in[j]), wb, a_tiles, wc, row(b_d[j]), bf(b_w_glu[j]))
        elif m == 2:
            x = _retention(x, g1, bf(c_w_in[j]), cos, sin, row(c_gn_g[j]), bf(c_w_out[j]))
        else:
            x = _sconv(x, g1, bf(d_w_in[j]), d_conv_w[j].reshape(CONV_WIDTH, -1).astype(F32),
                       bf(d_w_out[j]))
        x = _mlp(x.reshape(nbatch * l, d), row(norm2_g[i]), bf(mlp_w1[i]), bf(mlp_w2[i]),
                 row(final_g), final_norm=(i == depth - 1)).reshape(nbatch, l, d)
    return x
```

```python
import functools
import math

import jax
import jax.numpy as jnp
from jax import lax
from jax.experimental import pallas as pl
from jax.experimental.pallas import tpu as pltpu

F32 = jnp.float32
BF16 = jnp.bfloat16

EPS = 1e-6
LN_EPS = 1e-5
CHUNK = 128
GM_GROUPS = 8
S5_GROUP = 16
S5_STATE = 64
S5_COLBLOCK_GROUPS = 16
RET_HEADS = 4
ROPE_BASE = 10000.0
CONV_WIDTH = 3

LANES = 128
SUBLANES = 8
VMEM_LIMIT = 56 * 1024 * 1024


def _dot(a, b):
    return jnp.dot(a, b, preferred_element_type=F32)


def _rms(x, g):
    return x * lax.rsqrt(jnp.mean(x * x, axis=-1, keepdims=True) + EPS) * g


def _gelu(x):
    c = math.sqrt(2.0 / math.pi)
    return x * (0.5 * (1.0 + jnp.tanh(c * (x + 0.044715 * (x * x * x)))))


def _sigmoid(x):
    return 1.0 / (1.0 + jnp.exp(-x))


def _resident(shape):
    nd = len(shape)
    return pl.BlockSpec(shape, lambda *_: (0,) * nd, pipeline_mode=pl.Buffered(1))


def _params(semantics):
    return pltpu.CompilerParams(dimension_semantics=semantics, vmem_limit_bytes=VMEM_LIMIT)


def _mlp_kernel(x_ref, g_ref, w1_ref, w2_ref, fg_ref, o_ref, hn_scr, a_scr, *, final_norm, nf):
    dff = w1_ref.shape[1]
    hn_scr[...] = _rms(x_ref[...], g_ref[...]).astype(BF16)
    for n0 in range(0, dff, nf):
        a = jnp.maximum(_dot(hn_scr[...], w1_ref[:, n0:n0 + nf]), 0.0)
        a_scr[:, n0:n0 + nf] = (a * a).astype(BF16)
    y = x_ref[...] + _dot(a_scr[...], w2_ref[...])
    if final_norm:
        y = _rms(y, fg_ref[...])
    o_ref[...] = y


def _layer_resident(shape, layer):
    nd = len(shape) - 1
    return pl.BlockSpec((None,) + tuple(shape[1:]), lambda *_: (layer,) + (0,) * nd,
                        pipeline_mode=pl.Buffered(1))


def _mlp(x2, g, w1_all, w2_all, final_g, *, layer, final_norm, tm=512, nf=512):
    n, d = x2.shape
    dff = w1_all.shape[2]
    tm = min(tm, n)
    return pl.pallas_call(
        functools.partial(_mlp_kernel, final_norm=final_norm, nf=nf),
        grid=(n // tm,),
        in_specs=[
            pl.BlockSpec((tm, d), lambda i: (i, 0)),
            _resident((1, d)),
            _layer_resident(w1_all.shape, layer),
            _layer_resident(w2_all.shape, layer),
            _resident((1, d)),
        ],
        out_specs=pl.BlockSpec((tm, d), lambda i: (i, 0)),
        out_shape=jax.ShapeDtypeStruct((n, d), F32),
        scratch_shapes=[pltpu.VMEM((tm, d), BF16), pltpu.VMEM((tm, dff), BF16)],
        compiler_params=_params(("parallel",)),
        name="mlp_final" if final_norm else "mlp",
    )(x2, g, w1_all, w2_all, final_g)


def _gmlp_kernel(x_ref, g_ref, win_ref, lng_ref, ws_ref, bs_ref, wout_ref, o_ref,
                 hn_scr, u_scr, v_scr, vb_scr, gated_scr):
    tc = x_ref.shape[0]
    e = lng_ref.shape[1]
    n_groups = ws_ref.shape[0]
    gd = e // n_groups
    nb = 512
    hn_scr[...] = _rms(x_ref[...], g_ref[...]).astype(BF16)
    for c0 in range(0, e, nb):
        u_scr[:, c0:c0 + nb] = _gelu(_dot(hn_scr[...], win_ref[:, c0:c0 + nb]))
        v_scr[:, c0:c0 + nb] = _gelu(_dot(hn_scr[...], win_ref[:, e + c0:e + c0 + nb]))
    v = v_scr[...]
    mu = jnp.mean(v, axis=-1, keepdims=True)
    dv = v - mu
    var = jnp.mean(dv * dv, axis=-1, keepdims=True)
    vb_scr[...] = (dv * lax.rsqrt(var + LN_EPS) * lng_ref[...]).astype(BF16)
    row = lax.broadcasted_iota(jnp.int32, (CHUNK, CHUNK), 0)
    col = lax.broadcasted_iota(jnp.int32, (CHUNK, CHUNK), 1)
    for g in range(n_groups):
        ws = jnp.where(row >= col, ws_ref[g], 0.0).astype(BF16)
        cols = slice(g * gd, (g + 1) * gd)
        for r0 in range(0, tc, CHUNK):
            rows = slice(r0, r0 + CHUNK)
            sv = _dot(ws, vb_scr[rows, cols]) + bs_ref[g]
            gated_scr[rows, cols] = (u_scr[rows, cols] * sv).astype(BF16)
    o_ref[...] = x_ref[...] + _dot(gated_scr[...], wout_ref[...])


def _gmlp(x2, g, w_in, ln_g, ws, bs_b, w_out, *, tc=512):
    n, d = x2.shape
    e = ln_g.shape[1]
    tc = min(tc, n)
    return pl.pallas_call(
        _gmlp_kernel,
        grid=(n // tc,),
        in_specs=[
            pl.BlockSpec((tc, d), lambda i: (i, 0)),
            _resident((1, d)),
            _resident(w_in.shape),
            _resident((1, e)),
            _resident(ws.shape),
            _resident(bs_b.shape),
            _resident(w_out.shape),
        ],
        out_specs=pl.BlockSpec((tc, d), lambda i: (i, 0)),
        out_shape=jax.ShapeDtypeStruct((n, d), F32),
        scratch_shapes=[pltpu.VMEM((tc, d), BF16),
                        pltpu.VMEM((tc, e), F32), pltpu.VMEM((tc, e), F32),
                        pltpu.VMEM((tc, e), BF16), pltpu.VMEM((tc, e), BF16)],
        compiler_params=_params(("parallel",)),
        name="gmlp_mixer",
    )(x2, g, w_in, ln_g, ws, bs_b, w_out)


def _s5_pitch(tc):
    assert tc % SUBLANES == 0
    return tc + SUBLANES // 2


def _s5_kernel(x_ref, g_ref, win_ref, wb_ref, a_ref, wc_ref, d_ref, wglu_ref, o_ref,
               hn_scr, u_scr, h_scr, hc_scr, z_scr, *s_scr, pitch):
    nbatch, tc, d = x_ref.shape
    e = d_ref.shape[1]
    ncb = wb_ref.shape[0]
    cw = S5_COLBLOCK_GROUPS * S5_GROUP
    ns = S5_COLBLOCK_GROUPS * S5_STATE
    nslab = ns // LANES

    @pl.when(pl.program_id(0) == 0)
    def _():
        h_scr[...] = jnp.zeros_like(h_scr)

    nw = 2 * LANES
    ar = [a_ref[0, cb] for cb in range(ncb)]
    ai = [a_ref[1, cb] for cb in range(ncb)]
    state = [[h_scr[b * 2 * ncb + i] for i in range(2 * ncb)] for b in range(nbatch)]


    def front_units(b):
        units = []

        def norm():
            hn_scr[b] = _rms(x_ref[b], g_ref[...]).astype(BF16)
        units.append(norm)
        for c0 in range(0, e, 2 * nw):
            def proj(c0=c0):
                u_scr[b, :, c0:c0 + 2 * nw] = _dot(hn_scr[b], win_ref[:, c0:c0 + 2 * nw])
            units.append(proj)
        for cb in range(ncb):
            for c0 in range(0, 2 * ns, nw):
                def bu_unit(cb=cb, c0=c0):
                    ub = u_scr[b, :, cb * cw:(cb + 1) * cw].astype(BF16)
                    bu = _dot(ub, wb_ref[cb, :, c0:c0 + nw])
                    part, k0 = c0 // ns, (c0 % ns) // LANES
                    for kk in range(nw // LANES):
                        r0 = (k0 + kk) * pitch
                        s_scr[b][cb * 2 + part, r0:r0 + tc, :] = bu[:, kk * LANES:(kk + 1) * LANES]
                units.append(bu_unit)
        return units

    def scan_units(b, group=8):
        def make(t0):
            def run():
                for t in range(t0, t0 + group):
                    sl = pl.ds(t, SUBLANES, stride=pitch)
                    for cb in range(ncb):
                        hr, hi = state[b][2 * cb], state[b][2 * cb + 1]
                        nr = ar[cb] * hr - ai[cb] * hi + s_scr[b][2 * cb, sl, :]
                        ni = ar[cb] * hi + ai[cb] * hr + s_scr[b][2 * cb + 1, sl, :]
                        s_scr[b][2 * cb, sl, :] = nr
                        s_scr[b][2 * cb + 1, sl, :] = ni
                        state[b][2 * cb], state[b][2 * cb + 1] = nr, ni
            return run
        return [make(t0) for t0 in range(0, tc, group)]

    def back_units(b):
        units = []
        for cb in range(ncb):
            def gather(cb=cb):
                for part in range(2):
                    for k in range(nslab):
                        c0 = part * ns + k * LANES
                        hc_scr[b, cb % 2, :, c0:c0 + LANES] = (
                            s_scr[b][cb * 2 + part, k * pitch:k * pitch + tc, :].astype(BF16))
            units.append(gather)

            def out_proj(cb=cb):
                cols = slice(cb * cw, (cb + 1) * cw)
                y = _dot(hc_scr[b, cb % 2], wc_ref[cb]) + d_ref[:, cols] * u_scr[b, :, cols]
                z_scr[b, :, cols] = _gelu(y).astype(BF16)
            units.append(out_proj)
        for c0 in range(0, d, nw):
            def glu(c0=c0):
                val = _dot(z_scr[b], wglu_ref[:, c0:c0 + nw])
                gate = _dot(z_scr[b], wglu_ref[:, d + c0:d + c0 + nw])
                o_ref[b, :, c0:c0 + nw] = x_ref[b, :, c0:c0 + nw] + val * _sigmoid(gate)
            units.append(glu)
        return units

    def interleave(xs, ys):
        out, j = [], 0
        for i, unit in enumerate(xs):
            out.append(unit)
            upto = (i + 1) * len(ys) // len(xs)
            out += ys[j:upto]
            j = upto
        return out + ys[j:]

    for slot in range(nbatch + 2):
        mxu = []
        if slot < nbatch:
            mxu += front_units(slot)
        if 0 <= slot - 2 < nbatch:
            mxu += back_units(slot - 2)
        vpu = scan_units(slot - 1) if 0 <= slot - 1 < nbatch else []
        for unit in (interleave(mxu, vpu) if mxu else vpu):
            unit()
    for b in range(nbatch):
        for i in range(2 * ncb):
            h_scr[b * 2 * ncb + i] = state[b][i]


def _s5(x, g, w_in, wb, a_tiles, wc, d_skip, w_glu, *, tc=256):
    nbatch, l, d = x.shape
    e = d_skip.shape[1]
    ncb = wb.shape[0]
    tc = min(tc, l)
    pitch = _s5_pitch(tc)
    nslab = S5_COLBLOCK_GROUPS * S5_STATE // LANES
    return pl.pallas_call(
        functools.partial(_s5_kernel, pitch=pitch),
        grid=(l // tc,),
        in_specs=[
            pl.BlockSpec((nbatch, tc, d), lambda i: (0, i, 0)),
            _resident((1, d)),
            _resident(w_in.shape),
            _resident(wb.shape),
            _resident(a_tiles.shape),
            _resident(wc.shape),
            _resident((1, e)),
            _resident(w_glu.shape),
        ],
        out_specs=pl.BlockSpec((nbatch, tc, d), lambda i: (0, i, 0)),
        out_shape=jax.ShapeDtypeStruct((nbatch, l, d), F32),
        scratch_shapes=[
            pltpu.VMEM((nbatch, tc, d), BF16),
            pltpu.VMEM((nbatch, tc, e), F32),
            pltpu.VMEM((nbatch * ncb * 2, SUBLANES, LANES), F32),
            pltpu.VMEM((nbatch, 2, tc, 2 * nslab * LANES), BF16),
            pltpu.VMEM((nbatch, tc, e), BF16),
        ] + [pltpu.VMEM((ncb * 2, nslab * pitch, LANES), F32) for _ in range(nbatch)],
        compiler_params=_params(("arbitrary",)),
        name="s5_mixer",
    )(x, g, w_in, wb, a_tiles, wc, d_skip, w_glu)


def _s5_prepare(log_dt, a_re, a_im, b_re, b_im, c_re, c_im):
    ngroups, nstate = a_re.shape
    m = b_re.shape[2]
    dt = jnp.exp(log_dt.astype(F32))[:, None]
    ar, ai = a_re.astype(F32), a_im.astype(F32)
    mag = jnp.exp(ar * dt)
    ang = ai * dt
    abar_r, abar_i = mag * jnp.cos(ang), mag * jnp.sin(ang)
    den = ar * ar + ai * ai
    nr, ni = abar_r - 1.0, abar_i
    cr = (nr * ar + ni * ai) / den
    ci = (ni * ar - nr * ai) / den
    br, bi = b_re.astype(F32), b_im.astype(F32)
    bbar_r = cr[..., None] * br - ci[..., None] * bi
    bbar_i = cr[..., None] * bi + ci[..., None] * br
    gpb = S5_COLBLOCK_GROUPS
    ncb = ngroups // gpb
    eye = jnp.eye(gpb, dtype=F32)

    def blockdiag_in(w):
        w = w.reshape(ncb, gpb, nstate, m)
        return jnp.einsum('cgpm,gh->cgmhp', w, eye).reshape(ncb, gpb * m, gpb * nstate)

    def blockdiag_out(w):
        w = w.reshape(ncb, gpb, m, nstate)
        return jnp.einsum('cgmp,gh->cgphm', w, eye).reshape(ncb, gpb * nstate, gpb * m)

    wb = jnp.concatenate([blockdiag_in(bbar_r), blockdiag_in(bbar_i)], axis=2).astype(BF16)
    wc = jnp.concatenate([blockdiag_out(c_re.astype(F32)),
                          blockdiag_out(-c_im.astype(F32))], axis=1).astype(BF16)
    a_tiles = jnp.stack([abar_r, abar_i]).reshape(2, ncb, SUBLANES, LANES)
    return wb, a_tiles, wc


def _ret_kernel(x_ref, g_ref, win_ref, cc_ref, sc_ref, ct_ref, st_ref, gng_ref, wout_ref, o_ref,
                hn_scr, r_scr, gated_scr):
    nbatch, tc, d = x_ref.shape
    nheads = RET_HEADS
    dk = r_scr.shape[1]
    dv = r_scr.shape[2]
    qd = nheads * dk
    vd = nheads * dv
    half = dk // 2

    @pl.when(pl.program_id(0) == 0)
    def _():
        r_scr[...] = jnp.zeros_like(r_scr)

    for b in range(nbatch):
        hn_scr[b * tc:(b + 1) * tc, :] = _rms(x_ref[b], g_ref[...]).astype(BF16)

    cc, sc, ct, st = cc_ref[...], sc_ref[...], ct_ref[...], st_ref[...]
    cos = jnp.concatenate([cc * ct - sc * st] * nbatch, axis=0)
    sin = jnp.concatenate([sc * ct + cc * st] * nbatch, axis=0)
    row = lax.broadcasted_iota(jnp.int32, (tc, tc), 0)
    col = lax.broadcasted_iota(jnp.int32, (tc, tc), 1)
    diff = (row - col).astype(F32)
    t_v = lax.broadcasted_iota(jnp.int32, (tc, dv), 0).astype(F32)
    t_k = lax.broadcasted_iota(jnp.int32, (tc, dk), 0).astype(F32)

    def rotate(t):
        t1, t2 = t[:, :half], t[:, half:]
        return jnp.concatenate([t1 * cos - t2 * sin, t1 * sin + t2 * cos], axis=1)

    for h in range(nheads):
        log_g = math.log(1.0 - 2.0 ** (-5.0 - h))
        dmask = jnp.where(diff >= 0, jnp.exp(jnp.maximum(diff, 0.0) * log_g), 0.0)
        xi = jnp.exp((t_v + 1.0) * log_g)
        zeta = jnp.exp((tc - 1.0 - t_k) * log_g)
        hn = hn_scr[...]
        q = rotate(_dot(hn, win_ref[:, h * dk:(h + 1) * dk])).astype(BF16)
        k = rotate(_dot(hn, win_ref[:, qd + h * dk:qd + (h + 1) * dk])) * (dk ** -0.5)
        v = _dot(hn, win_ref[:, 2 * qd + h * dv:2 * qd + (h + 1) * dv]).astype(BF16)
        gate = _dot(hn, win_ref[:, 2 * qd + vd + h * dv:2 * qd + vd + (h + 1) * dv])
        for b in range(nbatch):
            rows = slice(b * tc, (b + 1) * tc)
            kb = k[rows]
            s = lax.dot_general(q[rows], kb.astype(BF16), (((1,), (1,)), ((), ())),
                                preferred_element_type=F32) * dmask
            r_old = r_scr[b * nheads + h]
            o = _dot(s.astype(BF16), v[rows]) + _dot(q[rows], r_old.astype(BF16)) * xi
            r_scr[b * nheads + h] = math.exp(tc * log_g) * r_old + lax.dot_general(
                (kb * zeta).astype(BF16), v[rows], (((0,), (0,)), ((), ())),
                preferred_element_type=F32)
            mu = jnp.mean(o, axis=-1, keepdims=True)
            do = o - mu
            var = jnp.mean(do * do, axis=-1, keepdims=True)
            on = do * lax.rsqrt(var + LN_EPS) * gng_ref[:, h * dv:(h + 1) * dv]
            gb = gate[rows]
            gated_scr[rows, h * dv:(h + 1) * dv] = (gb * _sigmoid(gb) * on).astype(BF16)
    nw = 512
    for c0 in range(0, d, nw):
        res = _dot(gated_scr[...], wout_ref[:, c0:c0 + nw])
        for b in range(nbatch):
            o_ref[b, :, c0:c0 + nw] = x_ref[b, :, c0:c0 + nw] + res[b * tc:(b + 1) * tc]


def _retention(x, g, w_in, gn_g, w_out, *, tc=256):
    nbatch, l, d = x.shape
    tc = min(tc, l)
    dk = d // RET_HEADS
    dv = 2 * dk
    vd = RET_HEADS * dv
    inv_freq = ROPE_BASE ** (-jnp.arange(0, dk, 2, dtype=F32) / dk)
    th_c = (jnp.arange(l // tc, dtype=F32) * tc)[:, None, None] * inv_freq
    th_t = jnp.arange(tc, dtype=F32)[:, None] * inv_freq
    chunk_spec = pl.BlockSpec((None, 1, dk // 2), lambda i: (i, 0, 0))
    return pl.pallas_call(
        _ret_kernel,
        grid=(l // tc,),
        in_specs=[
            pl.BlockSpec((nbatch, tc, d), lambda i: (0, i, 0)),
            _resident((1, d)),
            _resident(w_in.shape),
            chunk_spec,
            chunk_spec,
            _resident((tc, dk // 2)),
            _resident((tc, dk // 2)),
            _resident((1, vd)),
            _resident(w_out.shape),
        ],
        out_specs=pl.BlockSpec((nbatch, tc, d), lambda i: (0, i, 0)),
        out_shape=jax.ShapeDtypeStruct((nbatch, l, d), F32),
        scratch_shapes=[pltpu.VMEM((nbatch * tc, d), BF16),
                        pltpu.VMEM((nbatch * RET_HEADS, dk, dv), F32),
                        pltpu.VMEM((nbatch * tc, vd), BF16)],
        compiler_params=_params(("arbitrary",)),
        name="retention_mixer",
    )(x, g, w_in, jnp.cos(th_c), jnp.sin(th_c), jnp.cos(th_t), jnp.sin(th_t), gn_g, w_out)


def _sconv_kernel(x_ref, g_ref, win_ref, cw_ref, wout_ref, o_ref, hn_scr, zs_scr, gy_scr):
    nbatch, tc, d = x_ref.shape
    e = cw_ref.shape[1]
    pad = SUBLANES

    @pl.when(pl.program_id(0) == 0)
    def _():
        zs_scr[:, 0:pad, :] = jnp.zeros((nbatch, pad, e), F32)

    for b in range(nbatch):
        hn_scr[b * tc:(b + 1) * tc, :] = _rms(x_ref[b], g_ref[...]).astype(BF16)
    nw = 512
    for c0 in range(0, e, nw):
        cols = slice(c0, c0 + nw)
        hn = hn_scr[...]
        gate_b = _dot(hn, win_ref[:, c0:c0 + nw])
        z = _dot(hn, win_ref[:, e + c0:e + c0 + nw]) * _dot(hn, win_ref[:, 2 * e + c0:2 * e + c0 + nw])
        for b in range(nbatch):
            rows = slice(b * tc, (b + 1) * tc)
            zs_scr[b, pad:pad + tc, cols] = z[rows]
            y = (cw_ref[0:1, cols] * zs_scr[b, pad - 2:pad - 2 + tc, cols]
                 + cw_ref[1:2, cols] * zs_scr[b, pad - 1:pad - 1 + tc, cols]
                 + cw_ref[2:3, cols] * z[rows])
            zs_scr[b, 0:pad, cols] = zs_scr[b, tc:tc + pad, cols]
            gy_scr[rows, cols] = (gate_b[rows] * y).astype(BF16)
    for c0 in range(0, d, nw):
        res = _dot(gy_scr[...], wout_ref[:, c0:c0 + nw])
        for b in range(nbatch):
            o_ref[b, :, c0:c0 + nw] = x_ref[b, :, c0:c0 + nw] + res[b * tc:(b + 1) * tc]


def _sconv(x, g, w_in, conv_w, w_out, *, tc=256):
    nbatch, l, d = x.shape
    e = conv_w.shape[1]
    tc = min(tc, l)
    return pl.pallas_call(
        _sconv_kernel,
        grid=(l // tc,),
        in_specs=[
            pl.BlockSpec((nbatch, tc, d), lambda i: (0, i, 0)),
            _resident((1, d)),
            _resident(w_in.shape),
            _resident(conv_w.shape),
            _resident(w_out.shape),
        ],
        out_specs=pl.BlockSpec((nbatch, tc, d), lambda i: (0, i, 0)),
        out_shape=jax.ShapeDtypeStruct((nbatch, l, d), F32),
        scratch_shapes=[pltpu.VMEM((nbatch * tc, d), BF16),
                        pltpu.VMEM((nbatch, tc + SUBLANES, e), F32),
                        pltpu.VMEM((nbatch * tc, e), BF16)],
        compiler_params=_params(("arbitrary",)),
        name="sconv_mixer",
    )(x, g, w_in, conv_w, w_out)


def kernel(x, norm1_g, norm2_g, mlp_w1, mlp_w2, final_g, a_w_in, a_ln_g, a_ws, a_bs, a_w_out, b_w_in, b_log_dt, b_a_re, b_a_im, b_b_re, b_b_im, b_c_re, b_c_im, b_d, b_w_glu, c_w_in, c_gn_g, c_w_out, d_w_in, d_conv_w, d_w_out):
    nbatch, l, d = x.shape
    depth = norm1_g.shape[0]
    n_mixers = 4
    bf = lambda w: w.astype(BF16)
    row = lambda v: v.reshape(1, -1).astype(F32)
    w1_all, w2_all = bf(mlp_w1), bf(mlp_w2)

    for i in range(depth):
        m, j = i % n_mixers, i // n_mixers
        g1 = row(norm1_g[i])
        if m == 0:
            gd = a_ln_g.shape[1] // GM_GROUPS
            bs_b = jnp.broadcast_to(a_bs[j].astype(F32)[:, :, None], (GM_GROUPS, CHUNK, gd))
            x = _gmlp(x.reshape(nbatch * l, d), g1, bf(a_w_in[j]), row(a_ln_g[j]),
                      a_ws[j].astype(F32), bs_b, bf(a_w_out[j])).reshape(nbatch, l, d)
        elif m == 1:
            wb, a_tiles, wc = _s5_prepare(b_log_dt[j], b_a_re[j], b_a_im[j], b_b_re[j],
                                          b_b_im[j], b_c_re[j], b_c_im[j])
            x = _s5(x, g1, bf(b_w_in[j]), wb, a_tiles, wc, row(b_d[j]), bf(b_w_glu[j]))
        elif m == 2:
            x = _retention(x, g1, bf(c_w_in[j]), row(c_gn_g[j]), bf(c_w_out[j]))
        else:
            x = _sconv(x, g1, bf(d_w_in[j]), d_conv_w[j].reshape(CONV_WIDTH, -1).astype(F32),
                       bf(d_w_out[j]))
        x = _mlp(x.reshape(nbatch * l, d), row(norm2_g[i]), w1_all, w2_all, row(final_g),
                 layer=i, final_norm=(i == depth - 1)).reshape(nbatch, l, d)
    return x
```

```python
import functools
import math

import jax
import jax.numpy as jnp
from jax import lax
from jax.experimental import pallas as pl
from jax.experimental.pallas import tpu as pltpu

F32 = jnp.float32
BF16 = jnp.bfloat16

EPS = 1e-6
LN_EPS = 1e-5
CHUNK = 128
GM_GROUPS = 8
S5_GROUP = 16
S5_STATE = 64
S5_COLBLOCK_GROUPS = 16
RET_HEADS = 4
ROPE_BASE = 10000.0
CONV_WIDTH = 3

LANES = 128
SUBLANES = 8
VMEM_LIMIT = 56 * 1024 * 1024


def _dot(a, b):
    return jnp.dot(a, b, preferred_element_type=F32)


def _rms(x, g):
    return x * lax.rsqrt(jnp.mean(x * x, axis=-1, keepdims=True) + EPS) * g


def _gelu(x):
    c = math.sqrt(2.0 / math.pi)
    return x * (0.5 + 0.5 * jnp.tanh(x * (c + (c * 0.044715) * (x * x))))


def _sigmoid(x):
    return 1.0 / (1.0 + jnp.exp(-x))


def _interleave(xs, ys):
    if not xs:
        return list(ys)
    out, j = [], 0
    for i, unit in enumerate(xs):
        out.append(unit)
        upto = (i + 1) * len(ys) // len(xs)
        out += ys[j:upto]
        j = upto
    return out + ys[j:]


def _resident(shape):
    nd = len(shape)
    return pl.BlockSpec(shape, lambda *_: (0,) * nd, pipeline_mode=pl.Buffered(1))


def _params(semantics):
    return pltpu.CompilerParams(dimension_semantics=semantics, vmem_limit_bytes=VMEM_LIMIT)


def _launch(body, *, steps, inputs, in_specs, out_spec, out_shape, scratch, semantics, name,
            casts=()):
    n_in, ncast = len(inputs), len(casts)

    def kern(*refs):
        cast_in = refs[n_in:n_in + ncast]
        cast_out = refs[n_in + ncast + 1:n_in + 2 * ncast + 1]
        for src, dst in zip(cast_in, cast_out):
            dst[...] = src[...].astype(BF16)
        body(*refs[:n_in], refs[n_in + ncast], *refs[n_in + 2 * ncast + 1:])

    cast_in_specs, cast_out_specs, cast_shapes = [], [], []
    for w, layer in casts:
        _, rows, cols = w.shape
        rb = rows // steps
        assert rb * steps == rows and rb % (2 * SUBLANES) == 0, (w.shape, steps)
        cast_in_specs.append(pl.BlockSpec((None, rb, cols), lambda i, layer=layer: (layer, i, 0)))
        cast_out_specs.append(pl.BlockSpec((rb, cols), lambda i: (i, 0)))
        cast_shapes.append(jax.ShapeDtypeStruct((rows, cols), BF16))
    res = pl.pallas_call(
        kern,
        grid=(steps,),
        in_specs=list(in_specs) + cast_in_specs,
        out_specs=[out_spec] + cast_out_specs,
        out_shape=[out_shape] + cast_shapes,
        scratch_shapes=scratch,
        compiler_params=_params((semantics,)),
        name=name,
    )(*inputs, *[w for w, _ in casts])
    return res[0], list(res[1:])


def _mlp_kernel(x_ref, g_ref, w1_ref, w2_ref, fg_ref, o_ref, hn_scr, a_scr, *, final_norm, nf):
    dff = w1_ref.shape[1]
    hn_scr[...] = _rms(x_ref[...], g_ref[...]).astype(BF16)
    for n0 in range(0, dff, nf):
        a = jnp.maximum(_dot(hn_scr[...], w1_ref[:, n0:n0 + nf]), 0.0)
        a_scr[:, n0:n0 + nf] = (a * a).astype(BF16)
    y = x_ref[...] + _dot(a_scr[...], w2_ref[...])
    if final_norm:
        y = _rms(y, fg_ref[...])
    o_ref[...] = y


def _mlp(x2, g, w1, w2, final_g, *, final_norm, casts=(), tm=1024, nf=512):
    n, d = x2.shape
    dff = w1.shape[1]
    tm = min(tm, n)
    return _launch(
        functools.partial(_mlp_kernel, final_norm=final_norm, nf=nf),
        steps=n // tm,
        inputs=(x2, g, w1, w2, final_g),
        in_specs=[
            pl.BlockSpec((tm, d), lambda i: (i, 0)),
            _resident((1, d)),
            _resident(w1.shape),
            _resident(w2.shape),
            _resident((1, d)),
        ],
        out_spec=pl.BlockSpec((tm, d), lambda i: (i, 0)),
        out_shape=jax.ShapeDtypeStruct((n, d), F32),
        scratch=[pltpu.VMEM((tm, d), BF16), pltpu.VMEM((tm, dff), BF16)],
        semantics="parallel",
        name="mlp_final" if final_norm else "mlp",
        casts=casts,
    )


def _gmlp_kernel(x_ref, g_ref, win_ref, lng_ref, ws_ref, bs_ref, wout_ref, o_ref,
                 hn_scr, u_scr, v_scr, vb_scr, gated_scr):
    tc = x_ref.shape[0]
    e = lng_ref.shape[1]
    n_groups = ws_ref.shape[0]
    gd = e // n_groups
    nb = 256
    hn_scr[...] = _rms(x_ref[...], g_ref[...]).astype(BF16)
    for c0 in range(0, e, nb):
        v_scr[:, c0:c0 + nb] = _gelu(_dot(hn_scr[...], win_ref[:, e + c0:e + c0 + nb]))

    def u_unit(c0):
        u_scr[:, c0:c0 + nb] = _gelu(_dot(hn_scr[...], win_ref[:, c0:c0 + nb]))

    def ln_unit(r0):
        v = v_scr[r0:r0 + CHUNK, :]
        mu = jnp.mean(v, axis=-1, keepdims=True)
        dv = v - mu
        var = jnp.mean(dv * dv, axis=-1, keepdims=True)
        vb_scr[r0:r0 + CHUNK, :] = (dv * lax.rsqrt(var + LN_EPS) * lng_ref[...]).astype(BF16)

    for unit in _interleave([functools.partial(u_unit, c0) for c0 in range(0, e, nb)],
                            [functools.partial(ln_unit, r0) for r0 in range(0, tc, CHUNK)]):
        unit()
    row = lax.broadcasted_iota(jnp.int32, (CHUNK, CHUNK), 0)
    col = lax.broadcasted_iota(jnp.int32, (CHUNK, CHUNK), 1)
    for g in range(n_groups):
        ws = jnp.where(row >= col, ws_ref[g], 0.0).astype(BF16)
        cols = slice(g * gd, (g + 1) * gd)
        for r0 in range(0, tc, CHUNK):
            rows = slice(r0, r0 + CHUNK)
            sv = _dot(ws, vb_scr[rows, cols]) + bs_ref[g]
            gated_scr[rows, cols] = (u_scr[rows, cols] * sv).astype(BF16)
    o_ref[...] = x_ref[...] + _dot(gated_scr[...], wout_ref[...])


def _gmlp(x2, g, w_in, ln_g, ws, bs_b, w_out, *, casts=(), tc=512):
    n, d = x2.shape
    e = ln_g.shape[1]
    tc = min(tc, n)
    return _launch(
        _gmlp_kernel,
        steps=n // tc,
        inputs=(x2, g, w_in, ln_g, ws, bs_b, w_out),
        in_specs=[
            pl.BlockSpec((tc, d), lambda i: (i, 0)),
            _resident((1, d)),
            _resident(w_in.shape),
            _resident((1, e)),
            _resident(ws.shape),
            _resident(bs_b.shape),
            _resident(w_out.shape),
        ],
        out_spec=pl.BlockSpec((tc, d), lambda i: (i, 0)),
        out_shape=jax.ShapeDtypeStruct((n, d), F32),
        scratch=[pltpu.VMEM((tc, d), BF16),
                 pltpu.VMEM((tc, e), F32), pltpu.VMEM((tc, e), F32),
                 pltpu.VMEM((tc, e), BF16), pltpu.VMEM((tc, e), BF16)],
        semantics="parallel",
        name="gmlp_mixer",
        casts=casts,
    )


def _s5_pitch(tc):
    assert tc % SUBLANES == 0
    return tc + SUBLANES // 2


def _s5_kernel(x_ref, g_ref, win_ref, wb_ref, a_ref, wc_ref, d_ref, wglu_ref, o_ref,
               hn_scr, u_scr, h_scr, hc_scr, z_scr, *s_scr, pitch):
    nbatch, tc, d = x_ref.shape
    e = d_ref.shape[1]
    ncb = wb_ref.shape[0]
    cw = S5_COLBLOCK_GROUPS * S5_GROUP
    ns = S5_COLBLOCK_GROUPS * S5_STATE
    nslab = ns // LANES

    @pl.when(pl.program_id(0) == 0)
    def _():
        h_scr[...] = jnp.zeros_like(h_scr)

    nw = 2 * LANES
    ar = [a_ref[0, cb] for cb in range(ncb)]
    ai = [a_ref[1, cb] for cb in range(ncb)]
    state = [[h_scr[b * 2 * ncb + i] for i in range(2 * ncb)] for b in range(nbatch)]


    def front_units(b):
        units = []

        def norm():
            hn_scr[b] = _rms(x_ref[b], g_ref[...]).astype(BF16)
        units.append(norm)
        for c0 in range(0, e, 2 * nw):
            def proj(c0=c0):
                u_scr[b, :, c0:c0 + 2 * nw] = _dot(hn_scr[b], win_ref[:, c0:c0 + 2 * nw])
            units.append(proj)
        for cb in range(ncb):
            for c0 in range(0, 2 * ns, nw):
                def bu_unit(cb=cb, c0=c0):
                    ub = u_scr[b, :, cb * cw:(cb + 1) * cw].astype(BF16)
                    bu = _dot(ub, wb_ref[cb, :, c0:c0 + nw])
                    part, k0 = c0 // ns, (c0 % ns) // LANES
                    for kk in range(nw // LANES):
                        r0 = (k0 + kk) * pitch
                        s_scr[b][cb * 2 + part, r0:r0 + tc, :] = bu[:, kk * LANES:(kk + 1) * LANES]
                units.append(bu_unit)
        return units

    def scan_units(b, group=8):
        def make(t0):
            def run():
                for t in range(t0, t0 + group):
                    sl = pl.ds(t, SUBLANES, stride=pitch)
                    for cb in range(ncb):
                        hr, hi = state[b][2 * cb], state[b][2 * cb + 1]
                        nr = ar[cb] * hr - ai[cb] * hi + s_scr[b][2 * cb, sl, :]
                        ni = ar[cb] * hi + ai[cb] * hr + s_scr[b][2 * cb + 1, sl, :]
                        s_scr[b][2 * cb, sl, :] = nr
                        s_scr[b][2 * cb + 1, sl, :] = ni
                        state[b][2 * cb], state[b][2 * cb + 1] = nr, ni
            return run
        return [make(t0) for t0 in range(0, tc, group)]

    def back_units(b):
        units = []
        for cb in range(ncb):
            def gather(cb=cb):
                for part in range(2):
                    for k in range(nslab):
                        c0 = part * ns + k * LANES
                        hc_scr[b, cb % 2, :, c0:c0 + LANES] = (
                            s_scr[b][cb * 2 + part, k * pitch:k * pitch + tc, :].astype(BF16))
            units.append(gather)

            def out_proj(cb=cb):
                cols = slice(cb * cw, (cb + 1) * cw)
                y = _dot(hc_scr[b, cb % 2], wc_ref[cb]) + d_ref[:, cols] * u_scr[b, :, cols]
                z_scr[b, :, cols] = _gelu(y).astype(BF16)
            units.append(out_proj)
        for c0 in range(0, d, nw):
            def glu(c0=c0):
                val = _dot(z_scr[b], wglu_ref[:, c0:c0 + nw])
                gate = _dot(z_scr[b], wglu_ref[:, d + c0:d + c0 + nw])
                o_ref[b, :, c0:c0 + nw] = x_ref[b, :, c0:c0 + nw] + val * _sigmoid(gate)
            units.append(glu)
        return units

    for slot in range(nbatch + 2):
        mxu = []
        if slot < nbatch:
            mxu += front_units(slot)
        if 0 <= slot - 2 < nbatch:
            mxu += back_units(slot - 2)
        vpu = scan_units(slot - 1) if 0 <= slot - 1 < nbatch else []
        for unit in _interleave(mxu, vpu):
            unit()
    for b in range(nbatch):
        for i in range(2 * ncb):
            h_scr[b * 2 * ncb + i] = state[b][i]


def _s5(x, g, w_in, wb, a_tiles, wc, d_skip, w_glu, *, casts=(), tc=256):
    nbatch, l, d = x.shape
    e = d_skip.shape[1]
    ncb = wb.shape[0]
    tc = min(tc, l)
    pitch = _s5_pitch(tc)
    nslab = S5_COLBLOCK_GROUPS * S5_STATE // LANES
    return _launch(
        functools.partial(_s5_kernel, pitch=pitch),
        steps=l // tc,
        inputs=(x, g, w_in, wb, a_tiles, wc, d_skip, w_glu),
        in_specs=[
            pl.BlockSpec((nbatch, tc, d), lambda i: (0, i, 0)),
            _resident((1, d)),
            _resident(w_in.shape),
            _resident(wb.shape),
            _resident(a_tiles.shape),
            _resident(wc.shape),
            _resident((1, e)),
            _resident(w_glu.shape),
        ],
        out_spec=pl.BlockSpec((nbatch, tc, d), lambda i: (0, i, 0)),
        out_shape=jax.ShapeDtypeStruct((nbatch, l, d), F32),
        scratch=[
            pltpu.VMEM((nbatch, tc, d), BF16),
            pltpu.VMEM((nbatch, tc, e), F32),
            pltpu.VMEM((nbatch * ncb * 2, SUBLANES, LANES), F32),
            pltpu.VMEM((nbatch, 2, tc, 2 * nslab * LANES), BF16),
            pltpu.VMEM((nbatch, tc, e), BF16),
        ] + [pltpu.VMEM((ncb * 2, nslab * pitch, LANES), F32) for _ in range(nbatch)],
        semantics="arbitrary",
        name="s5_mixer",
        casts=casts,
    )


def _s5_prepare(log_dt, a_re, a_im, b_re, b_im, c_re, c_im):
    ngroups, nstate = a_re.shape
    m = b_re.shape[2]
    dt = jnp.exp(log_dt.astype(F32))[:, None]
    ar, ai = a_re.astype(F32), a_im.astype(F32)
    mag = jnp.exp(ar * dt)
    ang = ai * dt
    abar_r, abar_i = mag * jnp.cos(ang), mag * jnp.sin(ang)
    den = ar * ar + ai * ai
    nr, ni = abar_r - 1.0, abar_i
    cr = (nr * ar + ni * ai) / den
    ci = (ni * ar - nr * ai) / den
    br, bi = b_re.astype(F32), b_im.astype(F32)
    bbar_r = cr[..., None] * br - ci[..., None] * bi
    bbar_i = cr[..., None] * bi + ci[..., None] * br
    gpb = S5_COLBLOCK_GROUPS
    ncb = ngroups // gpb
    eye = jnp.eye(gpb, dtype=F32)

    def blockdiag(w):
        _, na, nb = w.shape
        wt = w.reshape(ncb, gpb, na, nb).transpose(0, 1, 3, 2)
        return (wt[:, :, :, None, :] * eye[None, :, None, :, None]).reshape(ncb, gpb * nb, gpb * na)

    wb = jnp.concatenate([blockdiag(bbar_r), blockdiag(bbar_i)], axis=2).astype(BF16)
    wc = jnp.concatenate([blockdiag(c_re.astype(F32)),
                          blockdiag(-c_im.astype(F32))], axis=1).astype(BF16)
    a_tiles = jnp.stack([abar_r, abar_i]).reshape(2, ncb, SUBLANES, LANES)
    return wb, a_tiles, wc


def _ret_kernel(x_ref, g_ref, win_ref, cc_ref, sc_ref, ct_ref, st_ref, gng_ref, wout_ref, o_ref,
                hn_scr, r_scr, gated_scr):
    nbatch, tc, d = x_ref.shape
    nheads = RET_HEADS
    dk = r_scr.shape[1]
    dv = r_scr.shape[2]
    qd = nheads * dk
    vd = nheads * dv
    half = dk // 2

    @pl.when(pl.program_id(0) == 0)
    def _():
        r_scr[...] = jnp.zeros_like(r_scr)

    for b in range(nbatch):
        hn_scr[b * tc:(b + 1) * tc, :] = _rms(x_ref[b], g_ref[...]).astype(BF16)

    cc, sc, ct, st = cc_ref[...], sc_ref[...], ct_ref[...], st_ref[...]
    cos = jnp.concatenate([cc * ct - sc * st] * nbatch, axis=0)
    sin = jnp.concatenate([sc * ct + cc * st] * nbatch, axis=0)
    row = lax.broadcasted_iota(jnp.int32, (tc, tc), 0)
    col = lax.broadcasted_iota(jnp.int32, (tc, tc), 1)
    diff = (row - col).astype(F32)
    t_v = lax.broadcasted_iota(jnp.int32, (tc, dv), 0).astype(F32)
    t_k = lax.broadcasted_iota(jnp.int32, (tc, dk), 0).astype(F32)

    def rotate(t):
        t1, t2 = t[:, :half], t[:, half:]
        return jnp.concatenate([t1 * cos - t2 * sin, t1 * sin + t2 * cos], axis=1)

    for h in range(nheads):
        log_g = math.log(1.0 - 2.0 ** (-5.0 - h))
        dmask = jnp.where(diff >= 0, jnp.exp(jnp.maximum(diff, 0.0) * log_g), 0.0)
        xi = jnp.exp((t_v + 1.0) * log_g)
        zeta = jnp.exp((tc - 1.0 - t_k) * log_g)
        hn = hn_scr[...]
        q = rotate(_dot(hn, win_ref[:, h * dk:(h + 1) * dk])).astype(BF16)
        k = rotate(_dot(hn, win_ref[:, qd + h * dk:qd + (h + 1) * dk])) * (dk ** -0.5)
        v = _dot(hn, win_ref[:, 2 * qd + h * dv:2 * qd + (h + 1) * dv]).astype(BF16)
        gate = _dot(hn, win_ref[:, 2 * qd + vd + h * dv:2 * qd + vd + (h + 1) * dv])
        for b in range(nbatch):
            rows = slice(b * tc, (b + 1) * tc)
            kb = k[rows]
            s = lax.dot_general(q[rows], kb.astype(BF16), (((1,), (1,)), ((), ())),
                                preferred_element_type=F32) * dmask
            r_old = r_scr[b * nheads + h]
            o = _dot(s.astype(BF16), v[rows]) + _dot(q[rows], r_old.astype(BF16)) * xi
            r_scr[b * nheads + h] = math.exp(tc * log_g) * r_old + lax.dot_general(
                (kb * zeta).astype(BF16), v[rows], (((0,), (0,)), ((), ())),
                preferred_element_type=F32)
            mu = jnp.mean(o, axis=-1, keepdims=True)
            do = o - mu
            var = jnp.mean(do * do, axis=-1, keepdims=True)
            on = do * lax.rsqrt(var + LN_EPS) * gng_ref[:, h * dv:(h + 1) * dv]
            gb = gate[rows]
            gated_scr[rows, h * dv:(h + 1) * dv] = (gb * _sigmoid(gb) * on).astype(BF16)
    nw = 512
    for c0 in range(0, d, nw):
        res = _dot(gated_scr[...], wout_ref[:, c0:c0 + nw])
        for b in range(nbatch):
            o_ref[b, :, c0:c0 + nw] = x_ref[b, :, c0:c0 + nw] + res[b * tc:(b + 1) * tc]


def _retention(x, g, w_in, gn_g, w_out, *, casts=(), tc=256):
    nbatch, l, d = x.shape
    tc = min(tc, l)
    dk = d // RET_HEADS
    dv = 2 * dk
    vd = RET_HEADS * dv
    inv_freq = ROPE_BASE ** (-jnp.arange(0, dk, 2, dtype=F32) / dk)
    th_c = (jnp.arange(l // tc, dtype=F32) * tc)[:, None, None] * inv_freq
    th_t = jnp.arange(tc, dtype=F32)[:, None] * inv_freq
    chunk_spec = pl.BlockSpec((None, 1, dk // 2), lambda i: (i, 0, 0))
    return _launch(
        _ret_kernel,
        steps=l // tc,
        inputs=(x, g, w_in, jnp.cos(th_c), jnp.sin(th_c), jnp.cos(th_t), jnp.sin(th_t), gn_g, w_out),
        in_specs=[
            pl.BlockSpec((nbatch, tc, d), lambda i: (0, i, 0)),
            _resident((1, d)),
            _resident(w_in.shape),
            chunk_spec,
            chunk_spec,
            _resident((tc, dk // 2)),
            _resident((tc, dk // 2)),
            _resident((1, vd)),
            _resident(w_out.shape),
        ],
        out_spec=pl.BlockSpec((nbatch, tc, d), lambda i: (0, i, 0)),
        out_shape=jax.ShapeDtypeStruct((nbatch, l, d), F32),
        scratch=[pltpu.VMEM((nbatch * tc, d), BF16),
                 pltpu.VMEM((nbatch * RET_HEADS, dk, dv), F32),
                 pltpu.VMEM((nbatch * tc, vd), BF16)],
        semantics="arbitrary",
        name="retention_mixer",
        casts=casts,
    )


def _sconv_kernel(x_ref, g_ref, win_ref, cw_ref, wout_ref, o_ref, hn_scr, zs_scr, gy_scr):
    nbatch, tc, d = x_ref.shape
    e = cw_ref.shape[1]
    pad = SUBLANES

    @pl.when(pl.program_id(0) == 0)
    def _():
        zs_scr[:, 0:pad, :] = jnp.zeros((nbatch, pad, e), F32)

    for b in range(nbatch):
        hn_scr[b * tc:(b + 1) * tc, :] = _rms(x_ref[b], g_ref[...]).astype(BF16)
    nw = 512
    for c0 in range(0, e, nw):
        cols = slice(c0, c0 + nw)
        hn = hn_scr[...]
        gate_b = _dot(hn, win_ref[:, c0:c0 + nw])
        z = _dot(hn, win_ref[:, e + c0:e + c0 + nw]) * _dot(hn, win_ref[:, 2 * e + c0:2 * e + c0 + nw])
        for b in range(nbatch):
            rows = slice(b * tc, (b + 1) * tc)
            zs_scr[b, pad:pad + tc, cols] = z[rows]
            y = (cw_ref[0:1, cols] * zs_scr[b, pad - 2:pad - 2 + tc, cols]
                 + cw_ref[1:2, cols] * zs_scr[b, pad - 1:pad - 1 + tc, cols]
                 + cw_ref[2:3, cols] * z[rows])
            zs_scr[b, 0:pad, cols] = zs_scr[b, tc:tc + pad, cols]
            gy_scr[rows, cols] = (gate_b[rows] * y).astype(BF16)
    for c0 in range(0, d, nw):
        res = _dot(gy_scr[...], wout_ref[:, c0:c0 + nw])
        for b in range(nbatch):
            o_ref[b, :, c0:c0 + nw] = x_ref[b, :, c0:c0 + nw] + res[b * tc:(b + 1) * tc]


def _sconv(x, g, w_in, conv_w, w_out, *, casts=(), tc=256):
    nbatch, l, d = x.shape
    e = conv_w.shape[1]
    tc = min(tc, l)
    return _launch(
        _sconv_kernel,
        steps=l // tc,
        inputs=(x, g, w_in, conv_w, w_out),
        in_specs=[
            pl.BlockSpec((nbatch, tc, d), lambda i: (0, i, 0)),
            _resident((1, d)),
            _resident(w_in.shape),
            _resident(conv_w.shape),
            _resident(w_out.shape),
        ],
        out_spec=pl.BlockSpec((nbatch, tc, d), lambda i: (0, i, 0)),
        out_shape=jax.ShapeDtypeStruct((nbatch, l, d), F32),
        scratch=[pltpu.VMEM((nbatch * tc, d), BF16),
                 pltpu.VMEM((nbatch, tc + SUBLANES, e), F32),
                 pltpu.VMEM((nbatch * tc, e), BF16)],
        semantics="arbitrary",
        name="sconv_mixer",
        casts=casts,
    )


def kernel(x, norm1_g, norm2_g, mlp_w1, mlp_w2, final_g, a_w_in, a_ln_g, a_ws, a_bs, a_w_out, b_w_in, b_log_dt, b_a_re, b_a_im, b_b_re, b_b_im, b_c_re, b_c_im, b_d, b_w_glu, c_w_in, c_gn_g, c_w_out, d_w_in, d_conv_w, d_w_out):
    nbatch, l, d = x.shape
    depth = norm1_g.shape[0]
    n_mixers = 4
    row = lambda v: v.reshape(1, -1).astype(F32)

    def mixer_weights(i):
        m, j = i % n_mixers, i // n_mixers
        pairs = ((a_w_in, a_w_out), (b_w_in, b_w_glu), (c_w_in, c_w_out), (d_w_in, d_w_out))[m]
        return [(w, j) for w in pairs]

    w_in, w_out = (w[j].astype(BF16) for w, j in mixer_weights(0))
    for i in range(depth):
        m, j = i % n_mixers, i // n_mixers
        g1 = row(norm1_g[i])
        mlp_casts = [(mlp_w1, i), (mlp_w2, i)]
        if m == 0:
            gd = a_ln_g.shape[1] // GM_GROUPS
            bs_b = jnp.broadcast_to(a_bs[j].astype(F32)[:, :, None], (GM_GROUPS, CHUNK, gd))
            x, (w1, w2) = _gmlp(x.reshape(nbatch * l, d), g1, w_in, row(a_ln_g[j]),
                                a_ws[j].astype(F32), bs_b, w_out, casts=mlp_casts)
        elif m == 1:
            wb, a_tiles, wc = _s5_prepare(b_log_dt[j], b_a_re[j], b_a_im[j], b_b_re[j],
                                          b_b_im[j], b_c_re[j], b_c_im[j])
            x, (w1, w2) = _s5(x, g1, w_in, wb, a_tiles, wc, row(b_d[j]), w_out, casts=mlp_casts)
        elif m == 2:
            x, (w1, w2) = _retention(x, g1, w_in, row(c_gn_g[j]), w_out, casts=mlp_casts)
        else:
            x, (w1, w2) = _sconv(x, g1, w_in, d_conv_w[j].reshape(CONV_WIDTH, -1).astype(F32),
                                 w_out, casts=mlp_casts)
        last = i == depth - 1
        x, nxt = _mlp(x.reshape(nbatch * l, d), row(norm2_g[i]), w1, w2, row(final_g),
                      final_norm=last, casts=() if last else mixer_weights(i + 1))
        x = x.reshape(nbatch, l, d)
        if not last:
            w_in, w_out = nxt
    return x
```

```python
import functools
import math

import jax
import jax.numpy as jnp
from jax import lax
from jax.experimental import pallas as pl
from jax.experimental.pallas import tpu as pltpu

F32 = jnp.float32
BF16 = jnp.bfloat16

EPS = 1e-6
LN_EPS = 1e-5
CHUNK = 128
GM_GROUPS = 8
S5_GROUP = 16
S5_STATE = 64
S5_COLBLOCK_GROUPS = 16
RET_HEADS = 4
ROPE_BASE = 10000.0
CONV_WIDTH = 3

LANES = 128
SUBLANES = 8
VMEM_LIMIT = 56 * 1024 * 1024


def _dot(a, b):
    return jnp.dot(a, b, preferred_element_type=F32)


def _rms(x, g):
    return x * lax.rsqrt(jnp.mean(x * x, axis=-1, keepdims=True) + EPS) * g


def _gelu(x):
    c = math.sqrt(2.0 / math.pi)
    return x * (0.5 + 0.5 * jnp.tanh(x * (c + (c * 0.044715) * (x * x))))


def _sigmoid(x):
    return 1.0 / (1.0 + jnp.exp(-x))


def _interleave(xs, ys):
    if not xs:
        return list(ys)
    out, j = [], 0
    for i, unit in enumerate(xs):
        out.append(unit)
        upto = (i + 1) * len(ys) // len(xs)
        out += ys[j:upto]
        j = upto
    return out + ys[j:]


def _resident(shape):
    nd = len(shape)
    return pl.BlockSpec(shape, lambda *_: (0,) * nd, pipeline_mode=pl.Buffered(1))


def _params(semantics):
    return pltpu.CompilerParams(dimension_semantics=semantics, vmem_limit_bytes=VMEM_LIMIT)


def _launch(body, *, steps, inputs, in_specs, out_spec, out_shape, scratch, semantics, name,
            casts=()):
    n_in, ncast = len(inputs), len(casts)

    def kern(*refs):
        cast_in = refs[n_in:n_in + ncast]
        cast_out = refs[n_in + ncast + 1:n_in + 2 * ncast + 1]
        for src, dst in zip(cast_in, cast_out):
            dst[...] = src[...].astype(BF16)
        body(*refs[:n_in], refs[n_in + ncast], *refs[n_in + 2 * ncast + 1:])

    cast_in_specs, cast_out_specs, cast_shapes = [], [], []
    for w, layer in casts:
        _, rows, cols = w.shape
        rb = rows // steps
        assert rb * steps == rows and rb % (2 * SUBLANES) == 0, (w.shape, steps)
        cast_in_specs.append(pl.BlockSpec((None, rb, cols), lambda i, layer=layer: (layer, i, 0)))
        cast_out_specs.append(pl.BlockSpec((rb, cols), lambda i: (i, 0)))
        cast_shapes.append(jax.ShapeDtypeStruct((rows, cols), BF16))
    res = pl.pallas_call(
        kern,
        grid=(steps,),
        in_specs=list(in_specs) + cast_in_specs,
        out_specs=[out_spec] + cast_out_specs,
        out_shape=[out_shape] + cast_shapes,
        scratch_shapes=scratch,
        compiler_params=_params((semantics,)),
        name=name,
    )(*inputs, *[w for w, _ in casts])
    return res[0], list(res[1:])


def _mlp_kernel(x_ref, g_ref, w1_ref, w2_ref, fg_ref, o_ref, hn_scr, a_scr, *,
                final_norm, nf, tr):
    dff = w1_ref.shape[1]
    for r0 in range(0, x_ref.shape[0], tr):
        rows = slice(r0, r0 + tr)
        hn_scr[rows, :] = _rms(x_ref[rows, :], g_ref[...]).astype(BF16)
        for n0 in range(0, dff, nf):
            a = jnp.maximum(_dot(hn_scr[rows, :], w1_ref[:, n0:n0 + nf]), 0.0)
            a_scr[rows, n0:n0 + nf] = (a * a).astype(BF16)
    for r0 in range(0, x_ref.shape[0], tr):
        rows = slice(r0, r0 + tr)
        y = x_ref[rows, :] + _dot(a_scr[rows, :], w2_ref[...])
        if final_norm:
            y = _rms(y, fg_ref[...])
        o_ref[rows, :] = y


def _mlp(x2, g, w1, w2, final_g, *, final_norm, casts=(), tm=1024, tr=512, nf=512):
    n, d = x2.shape
    dff = w1.shape[1]
    tm = min(tm, n)
    return _launch(
        functools.partial(_mlp_kernel, final_norm=final_norm, nf=nf, tr=min(tr, tm)),
        steps=n // tm,
        inputs=(x2, g, w1, w2, final_g),
        in_specs=[
            pl.BlockSpec((tm, d), lambda i: (i, 0)),
            _resident((1, d)),
            _resident(w1.shape),
            _resident(w2.shape),
            _resident((1, d)),
        ],
        out_spec=pl.BlockSpec((tm, d), lambda i: (i, 0)),
        out_shape=jax.ShapeDtypeStruct((n, d), F32),
        scratch=[pltpu.VMEM((tm, d), BF16), pltpu.VMEM((tm, dff), BF16)],
        semantics="parallel",
        name="mlp_final" if final_norm else "mlp",
        casts=casts,
    )


def _gmlp_kernel(x_ref, g_ref, win_ref, lng_ref, ws_ref, bs_ref, wout_ref, o_ref,
                 hn_scr, u_scr, v_scr, vb_scr, gated_scr, *, tr):
    tc = x_ref.shape[0]
    e = lng_ref.shape[1]
    n_groups = ws_ref.shape[0]
    gd = e // n_groups
    nb = 256
    row = lax.broadcasted_iota(jnp.int32, (CHUNK, CHUNK), 0)
    col = lax.broadcasted_iota(jnp.int32, (CHUNK, CHUNK), 1)
    ws = [jnp.where(row >= col, ws_ref[g], 0.0).astype(BF16) for g in range(n_groups)]

    def stages(r0):
        rows = slice(r0, r0 + tr)

        def norm():
            hn_scr[rows, :] = _rms(x_ref[rows, :], g_ref[...]).astype(BF16)

        def v_unit(c0):
            v_scr[rows, c0:c0 + nb] = _gelu(_dot(hn_scr[rows, :], win_ref[:, e + c0:e + c0 + nb]))

        def ln_unit(q0):
            v = v_scr[q0:q0 + CHUNK, :]
            mu = jnp.mean(v, axis=-1, keepdims=True)
            dv = v - mu
            var = jnp.mean(dv * dv, axis=-1, keepdims=True)
            vb_scr[q0:q0 + CHUNK, :] = (dv * lax.rsqrt(var + LN_EPS) * lng_ref[...]).astype(BF16)

        def u_unit(c0):
            u_scr[rows, c0:c0 + nb] = _gelu(_dot(hn_scr[rows, :], win_ref[:, c0:c0 + nb]))

        def mix_unit(g, q0):
            q, cols = slice(q0, q0 + CHUNK), slice(g * gd, (g + 1) * gd)
            sv = _dot(ws[g], vb_scr[q, cols]) + bs_ref[g]
            gated_scr[q, cols] = (u_scr[q, cols] * sv).astype(BF16)

        def out_unit(c0):
            o_ref[rows, c0:c0 + nb] = (x_ref[rows, c0:c0 + nb]
                                       + _dot(gated_scr[rows, :], wout_ref[:, c0:c0 + nb]))

        part = functools.partial
        chunks = range(r0, r0 + tr, CHUNK)
        return [
            [norm] + [part(v_unit, c0) for c0 in range(0, e, nb)],
            [part(ln_unit, q0) for q0 in chunks],
            [part(u_unit, c0) for c0 in range(0, e, nb)],
            [part(mix_unit, g, q0) for g in range(n_groups) for q0 in chunks],
            [part(out_unit, c0) for c0 in range(0, x_ref.shape[1], nb)],
        ]

    pipes = [stages(r0) for r0 in range(0, tc, tr)]
    nstage = len(pipes[0])
    for slot in range(nstage + len(pipes) - 1):
        units = []
        for p, pipe in enumerate(pipes):
            if 0 <= slot - p < nstage:
                units = _interleave(units, pipe[slot - p]) if units else list(pipe[slot - p])
        for unit in units:
            unit()


def _gmlp(x2, g, w_in, ln_g, ws, bs_b, w_out, *, casts=(), tc=512, tr=256):
    n, d = x2.shape
    e = ln_g.shape[1]
    tc = min(tc, n)
    return _launch(
        functools.partial(_gmlp_kernel, tr=min(tr, tc)),
        steps=n // tc,
        inputs=(x2, g, w_in, ln_g, ws, bs_b, w_out),
        in_specs=[
            pl.BlockSpec((tc, d), lambda i: (i, 0)),
            _resident((1, d)),
            _resident(w_in.shape),
            _resident((1, e)),
            _resident(ws.shape),
            _resident(bs_b.shape),
            _resident(w_out.shape),
        ],
        out_spec=pl.BlockSpec((tc, d), lambda i: (i, 0)),
        out_shape=jax.ShapeDtypeStruct((n, d), F32),
        scratch=[pltpu.VMEM((tc, d), BF16),
                 pltpu.VMEM((tc, e), F32), pltpu.VMEM((tc, e), F32),
                 pltpu.VMEM((tc, e), BF16), pltpu.VMEM((tc, e), BF16)],
        semantics="parallel",
        name="gmlp_mixer",
        casts=casts,
    )


def _s5_pitch(tc):
    assert tc % SUBLANES == 0
    return tc + SUBLANES // 2


def _s5_kernel(x_ref, g_ref, win_ref, wb_ref, a_ref, wc_ref, d_ref, wglu_ref, o_ref,
               hn_scr, u_scr, h_scr, hc_scr, z_scr, *s_scr, pitch):
    nbatch, tc, d = x_ref.shape
    e = d_ref.shape[1]
    ncb = wb_ref.shape[0]
    cw = S5_COLBLOCK_GROUPS * S5_GROUP
    ns = S5_COLBLOCK_GROUPS * S5_STATE
    nslab = ns // LANES

    @pl.when(pl.program_id(0) == 0)
    def _():
        h_scr[...] = jnp.zeros_like(h_scr)

    nw = 2 * LANES
    ar = [a_ref[0, cb] for cb in range(ncb)]
    ai = [a_ref[1, cb] for cb in range(ncb)]
    state = [[h_scr[b * 2 * ncb + i] for i in range(2 * ncb)] for b in range(nbatch)]


    def front_units(b):
        units = []

        def norm():
            hn_scr[b] = _rms(x_ref[b], g_ref[...]).astype(BF16)
        units.append(norm)
        for c0 in range(0, e, 2 * nw):
            def proj(c0=c0):
                u_scr[b, :, c0:c0 + 2 * nw] = _dot(hn_scr[b], win_ref[:, c0:c0 + 2 * nw])
            units.append(proj)
        for cb in range(ncb):
            for c0 in range(0, 2 * ns, nw):
                def bu_unit(cb=cb, c0=c0):
                    ub = u_scr[b, :, cb * cw:(cb + 1) * cw].astype(BF16)
                    bu = _dot(ub, wb_ref[cb, :, c0:c0 + nw])
                    part, k0 = c0 // ns, (c0 % ns) // LANES
                    for kk in range(nw // LANES):
                        r0 = (k0 + kk) * pitch
                        s_scr[b][cb * 2 + part, r0:r0 + tc, :] = bu[:, kk * LANES:(kk + 1) * LANES]
                units.append(bu_unit)
        return units

    def scan_units(b, group=8):
        def make(t0):
            def run():
                for t in range(t0, t0 + group):
                    sl = pl.ds(t, SUBLANES, stride=pitch)
                    for cb in range(ncb):
                        hr, hi = state[b][2 * cb], state[b][2 * cb + 1]
                        nr = ar[cb] * hr - ai[cb] * hi + s_scr[b][2 * cb, sl, :]
                        ni = ar[cb] * hi + ai[cb] * hr + s_scr[b][2 * cb + 1, sl, :]
                        s_scr[b][2 * cb, sl, :] = nr
                        s_scr[b][2 * cb + 1, sl, :] = ni
                        state[b][2 * cb], state[b][2 * cb + 1] = nr, ni
            return run
        return [make(t0) for t0 in range(0, tc, group)]

    def back_units(b):
        units = []
        for cb in range(ncb):
            def gather(cb=cb):
                for part in range(2):
                    for k in range(nslab):
                        c0 = part * ns + k * LANES
                        hc_scr[b, cb % 2, :, c0:c0 + LANES] = (
                            s_scr[b][cb * 2 + part, k * pitch:k * pitch + tc, :].astype(BF16))
            units.append(gather)

            def out_proj(cb=cb):
                cols = slice(cb * cw, (cb + 1) * cw)
                y = _dot(hc_scr[b, cb % 2], wc_ref[cb]) + d_ref[:, cols] * u_scr[b, :, cols]
                z_scr[b, :, cols] = _gelu(y).astype(BF16)
            units.append(out_proj)
        for c0 in range(0, d, nw):
            def glu(c0=c0):
                val = _dot(z_scr[b], wglu_ref[:, c0:c0 + nw])
                gate = _dot(z_scr[b], wglu_ref[:, d + c0:d + c0 + nw])
                o_ref[b, :, c0:c0 + nw] = x_ref[b, :, c0:c0 + nw] + val * _sigmoid(gate)
            units.append(glu)
        return units

    for slot in range(nbatch + 2):
        mxu = []
        if slot < nbatch:
            mxu += front_units(slot)
        if 0 <= slot - 2 < nbatch:
            mxu += back_units(slot - 2)
        vpu = scan_units(slot - 1) if 0 <= slot - 1 < nbatch else []
        for unit in _interleave(mxu, vpu):
            unit()
    for b in range(nbatch):
        for i in range(2 * ncb):
            h_scr[b * 2 * ncb + i] = state[b][i]


def _s5(x, g, w_in, wb, a_tiles, wc, d_skip, w_glu, *, casts=(), tc=256):
    nbatch, l, d = x.shape
    e = d_skip.shape[1]
    ncb = wb.shape[0]
    tc = min(tc, l)
    pitch = _s5_pitch(tc)
    nslab = S5_COLBLOCK_GROUPS * S5_STATE // LANES
    return _launch(
        functools.partial(_s5_kernel, pitch=pitch),
        steps=l // tc,
        inputs=(x, g, w_in, wb, a_tiles, wc, d_skip, w_glu),
        in_specs=[
            pl.BlockSpec((nbatch, tc, d), lambda i: (0, i, 0)),
            _resident((1, d)),
            _resident(w_in.shape),
            _resident(wb.shape),
            _resident(a_tiles.shape),
            _resident(wc.shape),
            _resident((1, e)),
            _resident(w_glu.shape),
        ],
        out_spec=pl.BlockSpec((nbatch, tc, d), lambda i: (0, i, 0)),
        out_shape=jax.ShapeDtypeStruct((nbatch, l, d), F32),
        scratch=[
            pltpu.VMEM((nbatch, tc, d), BF16),
            pltpu.VMEM((nbatch, tc, e), F32),
            pltpu.VMEM((nbatch * ncb * 2, SUBLANES, LANES), F32),
            pltpu.VMEM((nbatch, 2, tc, 2 * nslab * LANES), BF16),
            pltpu.VMEM((nbatch, tc, e), BF16),
        ] + [pltpu.VMEM((ncb * 2, nslab * pitch, LANES), F32) for _ in range(nbatch)],
        semantics="arbitrary",
        name="s5_mixer",
        casts=casts,
    )


def _s5_prepare(log_dt, a_re, a_im, b_re, b_im, c_re, c_im):
    ngroups, nstate = a_re.shape
    m = b_re.shape[2]
    dt = jnp.exp(log_dt.astype(F32))[:, None]
    ar, ai = a_re.astype(F32), a_im.astype(F32)
    mag = jnp.exp(ar * dt)
    ang = ai * dt
    abar_r, abar_i = mag * jnp.cos(ang), mag * jnp.sin(ang)
    den = ar * ar + ai * ai
    nr, ni = abar_r - 1.0, abar_i
    cr = (nr * ar + ni * ai) / den
    ci = (ni * ar - nr * ai) / den
    br, bi = b_re.astype(F32), b_im.astype(F32)
    bbar_r = cr[..., None] * br - ci[..., None] * bi
    bbar_i = cr[..., None] * bi + ci[..., None] * br
    gpb = S5_COLBLOCK_GROUPS
    ncb = ngroups // gpb

    def blockdiag(w):
        _, na, nb = w.shape
        wt = w.reshape(ncb, gpb, na, nb).transpose(0, 1, 3, 2).reshape(ncb, gpb * nb, na)
        r = lax.broadcasted_iota(jnp.int32, (gpb * nb, gpb * na), 0) // nb
        c = lax.broadcasted_iota(jnp.int32, (gpb * nb, gpb * na), 1) // na
        return jnp.where(r == c, jnp.tile(wt, (1, 1, gpb)), 0.0)

    wb = jnp.concatenate([blockdiag(bbar_r), blockdiag(bbar_i)], axis=2).astype(BF16)
    wc = jnp.concatenate([blockdiag(c_re.astype(F32)),
                          blockdiag(-c_im.astype(F32))], axis=1).astype(BF16)
    a_tiles = jnp.stack([abar_r, abar_i]).reshape(2, ncb, SUBLANES, LANES)
    return wb, a_tiles, wc


def _ret_kernel(x_ref, g_ref, win_ref, cc_ref, sc_ref, ct_ref, st_ref, gng_ref, wout_ref, o_ref,
                hn_scr, r_scr, gated_scr):
    nbatch, tc, d = x_ref.shape
    nheads = RET_HEADS
    dk = r_scr.shape[1]
    dv = r_scr.shape[2]
    qd = nheads * dk
    vd = nheads * dv
    half = dk // 2

    @pl.when(pl.program_id(0) == 0)
    def _():
        r_scr[...] = jnp.zeros_like(r_scr)

    for b in range(nbatch):
        hn_scr[b * tc:(b + 1) * tc, :] = _rms(x_ref[b], g_ref[...]).astype(BF16)

    cc, sc, ct, st = cc_ref[...], sc_ref[...], ct_ref[...], st_ref[...]
    cos = jnp.concatenate([cc * ct - sc * st] * nbatch, axis=0)
    sin = jnp.concatenate([sc * ct + cc * st] * nbatch, axis=0)
    row = lax.broadcasted_iota(jnp.int32, (tc, tc), 0)
    col = lax.broadcasted_iota(jnp.int32, (tc, tc), 1)
    diff = (row - col).astype(F32)
    t_v = lax.broadcasted_iota(jnp.int32, (tc, dv), 0).astype(F32)
    t_k = lax.broadcasted_iota(jnp.int32, (tc, dk), 0).astype(F32)

    def rotate(t):
        t1, t2 = t[:, :half], t[:, half:]
        return jnp.concatenate([t1 * cos - t2 * sin, t1 * sin + t2 * cos], axis=1)

    for h in range(nheads):
        log_g = math.log(1.0 - 2.0 ** (-5.0 - h))
        dmask = jnp.where(diff >= 0, jnp.exp(jnp.maximum(diff, 0.0) * log_g), 0.0)
        xi = jnp.exp((t_v + 1.0) * log_g)
        zeta = jnp.exp((tc - 1.0 - t_k) * log_g)
        hn = hn_scr[...]
        q = rotate(_dot(hn, win_ref[:, h * dk:(h + 1) * dk])).astype(BF16)
        k = rotate(_dot(hn, win_ref[:, qd + h * dk:qd + (h + 1) * dk])) * (dk ** -0.5)
        v = _dot(hn, win_ref[:, 2 * qd + h * dv:2 * qd + (h + 1) * dv]).astype(BF16)
        gate = _dot(hn, win_ref[:, 2 * qd + vd + h * dv:2 * qd + vd + (h + 1) * dv])
        for b in range(nbatch):
            rows = slice(b * tc, (b + 1) * tc)
            kb = k[rows]
            s = lax.dot_general(q[rows], kb.astype(BF16), (((1,), (1,)), ((), ())),
                                preferred_element_type=F32) * dmask
            r_old = r_scr[b * nheads + h]
            o = _dot(s.astype(BF16), v[rows]) + _dot(q[rows], r_old.astype(BF16)) * xi
            r_scr[b * nheads + h] = math.exp(tc * log_g) * r_old + lax.dot_general(
                (kb * zeta).astype(BF16), v[rows], (((0,), (0,)), ((), ())),
                preferred_element_type=F32)
            mu = jnp.mean(o, axis=-1, keepdims=True)
            do = o - mu
            var = jnp.mean(do * do, axis=-1, keepdims=True)
            on = do * lax.rsqrt(var + LN_EPS) * gng_ref[:, h * dv:(h + 1) * dv]
            gb = gate[rows]
            gated_scr[rows, h * dv:(h + 1) * dv] = (gb * _sigmoid(gb) * on).astype(BF16)
    nw = 512
    for c0 in range(0, d, nw):
        res = _dot(gated_scr[...], wout_ref[:, c0:c0 + nw])
        for b in range(nbatch):
            o_ref[b, :, c0:c0 + nw] = x_ref[b, :, c0:c0 + nw] + res[b * tc:(b + 1) * tc]


def _retention(x, g, w_in, gn_g, w_out, *, casts=(), tc=256):
    nbatch, l, d = x.shape
    tc = min(tc, l)
    dk = d // RET_HEADS
    dv = 2 * dk
    vd = RET_HEADS * dv
    inv_freq = ROPE_BASE ** (-jnp.arange(0, dk, 2, dtype=F32) / dk)
    th_c = (jnp.arange(l // tc, dtype=F32) * tc)[:, None, None] * inv_freq
    th_t = jnp.arange(tc, dtype=F32)[:, None] * inv_freq
    chunk_spec = pl.BlockSpec((None, 1, dk // 2), lambda i: (i, 0, 0))
    return _launch(
        _ret_kernel,
        steps=l // tc,
        inputs=(x, g, w_in, jnp.cos(th_c), jnp.sin(th_c), jnp.cos(th_t), jnp.sin(th_t), gn_g, w_out),
        in_specs=[
            pl.BlockSpec((nbatch, tc, d), lambda i: (0, i, 0)),
            _resident((1, d)),
            _resident(w_in.shape),
            chunk_spec,
            chunk_spec,
            _resident((tc, dk // 2)),
            _resident((tc, dk // 2)),
            _resident((1, vd)),
            _resident(w_out.shape),
        ],
        out_spec=pl.BlockSpec((nbatch, tc, d), lambda i: (0, i, 0)),
        out_shape=jax.ShapeDtypeStruct((nbatch, l, d), F32),
        scratch=[pltpu.VMEM((nbatch * tc, d), BF16),
                 pltpu.VMEM((nbatch * RET_HEADS, dk, dv), F32),
                 pltpu.VMEM((nbatch * tc, vd), BF16)],
        semantics="arbitrary",
        name="retention_mixer",
        casts=casts,
    )


def _sconv_kernel(x_ref, g_ref, win_ref, cw_ref, wout_ref, o_ref, hn_scr, zs_scr, gy_scr):
    nbatch, tc, d = x_ref.shape
    e = cw_ref.shape[1]
    pad = SUBLANES

    @pl.when(pl.program_id(0) == 0)
    def _():
        zs_scr[:, 0:pad, :] = jnp.zeros((nbatch, pad, e), F32)

    for b in range(nbatch):
        hn_scr[b * tc:(b + 1) * tc, :] = _rms(x_ref[b], g_ref[...]).astype(BF16)
    nw = 512
    for c0 in range(0, e, nw):
        cols = slice(c0, c0 + nw)
        hn = hn_scr[...]
        gate_b = _dot(hn, win_ref[:, c0:c0 + nw])
        z = _dot(hn, win_ref[:, e + c0:e + c0 + nw]) * _dot(hn, win_ref[:, 2 * e + c0:2 * e + c0 + nw])
        for b in range(nbatch):
            rows = slice(b * tc, (b + 1) * tc)
            zs_scr[b, pad:pad + tc, cols] = z[rows]
            y = (cw_ref[0:1, cols] * zs_scr[b, pad - 2:pad - 2 + tc, cols]
                 + cw_ref[1:2, cols] * zs_scr[b, pad - 1:pad - 1 + tc, cols]
                 + cw_ref[2:3, cols] * z[rows])
            zs_scr[b, 0:pad, cols] = zs_scr[b, tc:tc + pad, cols]
            gy_scr[rows, cols] = (gate_b[rows] * y).astype(BF16)
    for c0 in range(0, d, nw):
        res = _dot(gy_scr[...], wout_ref[:, c0:c0 + nw])
        for b in range(nbatch):
            o_ref[b, :, c0:c0 + nw] = x_ref[b, :, c0:c0 + nw] + res[b * tc:(b + 1) * tc]


def _sconv(x, g, w_in, conv_w, w_out, *, casts=(), tc=256):
    nbatch, l, d = x.shape
    e = conv_w.shape[1]
    tc = min(tc, l)
    return _launch(
        _sconv_kernel,
        steps=l // tc,
        inputs=(x, g, w_in, conv_w, w_out),
        in_specs=[
            pl.BlockSpec((nbatch, tc, d), lambda i: (0, i, 0)),
            _resident((1, d)),
            _resident(w_in.shape),
            _resident(conv_w.shape),
            _resident(w_out.shape),
        ],
        out_spec=pl.BlockSpec((nbatch, tc, d), lambda i: (0, i, 0)),
        out_shape=jax.ShapeDtypeStruct((nbatch, l, d), F32),
        scratch=[pltpu.VMEM((nbatch * tc, d), BF16),
                 pltpu.VMEM((nbatch, tc + SUBLANES, e), F32),
                 pltpu.VMEM((nbatch * tc, e), BF16)],
        semantics="arbitrary",
        name="sconv_mixer",
        casts=casts,
    )


def kernel(x, norm1_g, norm2_g, mlp_w1, mlp_w2, final_g, a_w_in, a_ln_g, a_ws, a_bs, a_w_out, b_w_in, b_log_dt, b_a_re, b_a_im, b_b_re, b_b_im, b_c_re, b_c_im, b_d, b_w_glu, c_w_in, c_gn_g, c_w_out, d_w_in, d_conv_w, d_w_out):
    nbatch, l, d = x.shape
    depth = norm1_g.shape[0]
    n_mixers = 4
    row = lambda v: v.reshape(1, -1).astype(F32)

    def mixer_weights(i):
        m, j = i % n_mixers, i // n_mixers
        pairs = ((a_w_in, a_w_out), (b_w_in, b_w_glu), (c_w_in, c_w_out), (d_w_in, d_w_out))[m]
        return [(w, j) for w in pairs]

    w_in, w_out = (w[j].astype(BF16) for w, j in mixer_weights(0))
    for i in range(depth):
        m, j = i % n_mixers, i // n_mixers
        g1 = row(norm1_g[i])
        mlp_casts = [(mlp_w1, i), (mlp_w2, i)]
        if m == 0:
            gd = a_ln_g.shape[1] // GM_GROUPS
            bs_b = jnp.broadcast_to(a_bs[j].astype(F32)[:, :, None], (GM_GROUPS, CHUNK, gd))
            x, (w1, w2) = _gmlp(x.reshape(nbatch * l, d), g1, w_in, row(a_ln_g[j]),
                                a_ws[j].astype(F32), bs_b, w_out, casts=mlp_casts)
        elif m == 1:
            wb, a_tiles, wc = _s5_prepare(b_log_dt[j], b_a_re[j], b_a_im[j], b_b_re[j],
                                          b_b_im[j], b_c_re[j], b_c_im[j])
            x, (w1, w2) = _s5(x, g1, w_in, wb, a_tiles, wc, row(b_d[j]), w_out, casts=mlp_casts)
        elif m == 2:
            x, (w1, w2) = _retention(x, g1, w_in, row(c_gn_g[j]), w_out, casts=mlp_casts)
        else:
            x, (w1, w2) = _sconv(x, g1, w_in, d_conv_w[j].reshape(CONV_WIDTH, -1).astype(F32),
                                 w_out, casts=mlp_casts)
        last = i == depth - 1
        x, nxt = _mlp(x.reshape(nbatch * l, d), row(norm2_g[i]), w1, w2, row(final_g),
                      final_norm=last, casts=() if last else mixer_weights(i + 1))
        x = x.reshape(nbatch, l, d)
        if not last:
            w_in, w_out = nxt
    return x
```

```python
import functools
import math

import jax
import jax.numpy as jnp
from jax import lax
from jax.experimental import pallas as pl
from jax.experimental.pallas import tpu as pltpu

F32 = jnp.float32
BF16 = jnp.bfloat16

EPS = 1e-6
LN_EPS = 1e-5
CHUNK = 128
GM_GROUPS = 8
S5_GROUP = 16
S5_STATE = 64
S5_COLBLOCK_GROUPS = 16
RET_HEADS = 4
ROPE_BASE = 10000.0
CONV_WIDTH = 3

LANES = 128
SUBLANES = 8
VMEM_LIMIT = 56 * 1024 * 1024


def _dot(a, b):
    return jnp.dot(a, b, preferred_element_type=F32)


def _rms(x, g):
    return x * lax.rsqrt(jnp.mean(x * x, axis=-1, keepdims=True) + EPS) * g


def _gelu(x):
    c = math.sqrt(2.0 / math.pi)
    return x * (0.5 + 0.5 * jnp.tanh(x * (c + (c * 0.044715) * (x * x))))


def _sigmoid(x):
    return 1.0 / (1.0 + jnp.exp(-x))


def _interleave(xs, ys):
    if not xs:
        return list(ys)
    out, j = [], 0
    for i, unit in enumerate(xs):
        out.append(unit)
        upto = (i + 1) * len(ys) // len(xs)
        out += ys[j:upto]
        j = upto
    return out + ys[j:]


def _resident(shape):
    nd = len(shape)
    return pl.BlockSpec(shape, lambda *_: (0,) * nd, pipeline_mode=pl.Buffered(1))


def _params(semantics):
    return pltpu.CompilerParams(dimension_semantics=semantics, vmem_limit_bytes=VMEM_LIMIT)


def _launch(body, *, steps, inputs, in_specs, out_spec, out_shape, scratch, semantics, name,
            casts=()):
    n_in, ncast = len(inputs), len(casts)

    def kern(*refs):
        cast_in = refs[n_in:n_in + ncast]
        cast_out = refs[n_in + ncast + 1:n_in + 2 * ncast + 1]
        for src, dst in zip(cast_in, cast_out):
            dst[...] = src[...].astype(BF16)
        body(*refs[:n_in], refs[n_in + ncast], *refs[n_in + 2 * ncast + 1:])

    cast_in_specs, cast_out_specs, cast_shapes = [], [], []
    for w, layer in casts:
        _, rows, cols = w.shape
        rb = rows // steps
        assert rb * steps == rows and rb % (2 * SUBLANES) == 0, (w.shape, steps)
        cast_in_specs.append(pl.BlockSpec((None, rb, cols), lambda i, layer=layer: (layer, i, 0)))
        cast_out_specs.append(pl.BlockSpec((rb, cols), lambda i: (i, 0)))
        cast_shapes.append(jax.ShapeDtypeStruct((rows, cols), BF16))
    res = pl.pallas_call(
        kern,
        grid=(steps,),
        in_specs=list(in_specs) + cast_in_specs,
        out_specs=[out_spec] + cast_out_specs,
        out_shape=[out_shape] + cast_shapes,
        scratch_shapes=scratch,
        compiler_params=_params((semantics,)),
        name=name,
    )(*inputs, *[w for w, _ in casts])
    return res[0], list(res[1:])


def _mlp_kernel(x_ref, g_ref, w1_ref, w2_ref, fg_ref, o_ref, hn_scr, a_scr, *,
                final_norm, nf, tr):
    dff = w1_ref.shape[1]
    for r0 in range(0, x_ref.shape[0], tr):
        rows = slice(r0, r0 + tr)
        hn_scr[rows, :] = _rms(x_ref[rows, :], g_ref[...]).astype(BF16)
        for n0 in range(0, dff, nf):
            a = jnp.maximum(_dot(hn_scr[rows, :], w1_ref[:, n0:n0 + nf]), 0.0)
            a_scr[rows, n0:n0 + nf] = (a * a).astype(BF16)
    for r0 in range(0, x_ref.shape[0], tr):
        rows = slice(r0, r0 + tr)
        y = x_ref[rows, :] + _dot(a_scr[rows, :], w2_ref[...])
        if final_norm:
            y = _rms(y, fg_ref[...])
        o_ref[rows, :] = y


def _mlp(x2, g, w1, w2, final_g, *, final_norm, casts=(), tm=1024, tr=512, nf=256):
    n, d = x2.shape
    dff = w1.shape[1]
    tm = min(tm, n)
    return _launch(
        functools.partial(_mlp_kernel, final_norm=final_norm, nf=nf, tr=min(tr, tm)),
        steps=n // tm,
        inputs=(x2, g, w1, w2, final_g),
        in_specs=[
            pl.BlockSpec((tm, d), lambda i: (i, 0)),
            _resident((1, d)),
            _resident(w1.shape),
            _resident(w2.shape),
            _resident((1, d)),
        ],
        out_spec=pl.BlockSpec((tm, d), lambda i: (i, 0)),
        out_shape=jax.ShapeDtypeStruct((n, d), F32),
        scratch=[pltpu.VMEM((tm, d), BF16), pltpu.VMEM((tm, dff), BF16)],
        semantics="parallel",
        name="mlp_final" if final_norm else "mlp",
        casts=casts,
    )


def _gmlp_kernel(x_ref, g_ref, win_ref, lng_ref, ws_ref, bs_ref, wout_ref, o_ref,
                 hn_scr, u_scr, v_scr, vb_scr, gated_scr, *, tr):
    tc = x_ref.shape[0]
    e = lng_ref.shape[1]
    n_groups = ws_ref.shape[0]
    gd = e // n_groups
    nb = 256
    row = lax.broadcasted_iota(jnp.int32, (CHUNK, CHUNK), 0)
    col = lax.broadcasted_iota(jnp.int32, (CHUNK, CHUNK), 1)
    ws = [jnp.where(row >= col, ws_ref[g], 0.0).astype(BF16) for g in range(n_groups)]

    def stages(r0):
        rows = slice(r0, r0 + tr)

        def norm():
            hn_scr[rows, :] = _rms(x_ref[rows, :], g_ref[...]).astype(BF16)

        def v_unit(c0):
            v_scr[rows, c0:c0 + nb] = _gelu(_dot(hn_scr[rows, :], win_ref[:, e + c0:e + c0 + nb]))

        def ln_unit(q0):
            v = v_scr[q0:q0 + CHUNK, :]
            mu = jnp.mean(v, axis=-1, keepdims=True)
            dv = v - mu
            var = jnp.mean(dv * dv, axis=-1, keepdims=True)
            vb_scr[q0:q0 + CHUNK, :] = (dv * lax.rsqrt(var + LN_EPS) * lng_ref[...]).astype(BF16)

        def u_unit(c0):
            u_scr[rows, c0:c0 + nb] = _gelu(_dot(hn_scr[rows, :], win_ref[:, c0:c0 + nb]))

        def mix_unit(g, q0):
            q, cols = slice(q0, q0 + CHUNK), slice(g * gd, (g + 1) * gd)
            sv = _dot(ws[g], vb_scr[q, cols]) + bs_ref[g]
            gated_scr[q, cols] = (u_scr[q, cols] * sv).astype(BF16)

        def out_unit(c0):
            o_ref[rows, c0:c0 + nb] = (x_ref[rows, c0:c0 + nb]
                                       + _dot(gated_scr[rows, :], wout_ref[:, c0:c0 + nb]))

        part = functools.partial
        chunks = range(r0, r0 + tr, CHUNK)
        return [
            [norm] + [part(v_unit, c0) for c0 in range(0, e, nb)],
            [part(ln_unit, q0) for q0 in chunks],
            [part(u_unit, c0) for c0 in range(0, e, nb)],
            [part(mix_unit, g, q0) for g in range(n_groups) for q0 in chunks],
            [part(out_unit, c0) for c0 in range(0, x_ref.shape[1], nb)],
        ]

    pipes = [stages(r0) for r0 in range(0, tc, tr)]
    nstage = len(pipes[0])
    for slot in range(nstage + len(pipes) - 1):
        units = []
        for p, pipe in enumerate(pipes):
            if 0 <= slot - p < nstage:
                units = _interleave(units, pipe[slot - p]) if units else list(pipe[slot - p])
        for unit in units:
            unit()


def _gmlp(x2, g, w_in, ln_g, ws, bs_b, w_out, *, casts=(), tc=512, tr=256):
    n, d = x2.shape
    e = ln_g.shape[1]
    tc = min(tc, n)
    return _launch(
        functools.partial(_gmlp_kernel, tr=min(tr, tc)),
        steps=n // tc,
        inputs=(x2, g, w_in, ln_g, ws, bs_b, w_out),
        in_specs=[
            pl.BlockSpec((tc, d), lambda i: (i, 0)),
            _resident((1, d)),
            _resident(w_in.shape),
            _resident((1, e)),
            _resident(ws.shape),
            _resident(bs_b.shape),
            _resident(w_out.shape),
        ],
        out_spec=pl.BlockSpec((tc, d), lambda i: (i, 0)),
        out_shape=jax.ShapeDtypeStruct((n, d), F32),
        scratch=[pltpu.VMEM((tc, d), BF16),
                 pltpu.VMEM((tc, e), F32), pltpu.VMEM((tc, e), F32),
                 pltpu.VMEM((tc, e), BF16), pltpu.VMEM((tc, e), BF16)],
        semantics="parallel",
        name="gmlp_mixer",
        casts=casts,
    )


def _s5_pitch(tc):
    assert tc % SUBLANES == 0
    return tc + SUBLANES // 2


def _s5_kernel(x_ref, g_ref, win_ref, wb_ref, a_ref, wc_ref, d_ref, wglu_ref, o_ref,
               hn_scr, u_scr, h_scr, hc_scr, z_scr, *s_scr, pitch):
    nbatch, tc, d = x_ref.shape
    e = d_ref.shape[1]
    ncb = wb_ref.shape[0]
    cw = S5_COLBLOCK_GROUPS * S5_GROUP
    ns = S5_COLBLOCK_GROUPS * S5_STATE
    nslab = ns // LANES

    @pl.when(pl.program_id(0) == 0)
    def _():
        h_scr[...] = jnp.zeros_like(h_scr)

    nw = 2 * LANES
    ar = [a_ref[0, cb] for cb in range(ncb)]
    ai = [a_ref[1, cb] for cb in range(ncb)]
    state = [[h_scr[b * 2 * ncb + i] for i in range(2 * ncb)] for b in range(nbatch)]


    def front_units(b):
        units = []

        def norm():
            hn_scr[b] = _rms(x_ref[b], g_ref[...]).astype(BF16)
        units.append(norm)
        for c0 in range(0, e, 2 * nw):
            def proj(c0=c0):
                u_scr[b, :, c0:c0 + 2 * nw] = _dot(hn_scr[b], win_ref[:, c0:c0 + 2 * nw])
            units.append(proj)
        for cb in range(ncb):
            for c0 in range(0, 2 * ns, nw):
                def bu_unit(cb=cb, c0=c0):
                    ub = u_scr[b, :, cb * cw:(cb + 1) * cw].astype(BF16)
                    bu = _dot(ub, wb_ref[cb, :, c0:c0 + nw])
                    part, k0 = c0 // ns, (c0 % ns) // LANES
                    for kk in range(nw // LANES):
                        r0 = (k0 + kk) * pitch
                        s_scr[b][cb * 2 + part, r0:r0 + tc, :] = bu[:, kk * LANES:(kk + 1) * LANES]
                units.append(bu_unit)
        return units

    def scan_units(b, group=8):
        def make(t0):
            def run():
                for t in range(t0, t0 + group):
                    sl = pl.ds(t, SUBLANES, stride=pitch)
                    for cb in range(ncb):
                        hr, hi = state[b][2 * cb], state[b][2 * cb + 1]
                        nr = ar[cb] * hr - ai[cb] * hi + s_scr[b][2 * cb, sl, :]
                        ni = ar[cb] * hi + ai[cb] * hr + s_scr[b][2 * cb + 1, sl, :]
                        s_scr[b][2 * cb, sl, :] = nr
                        s_scr[b][2 * cb + 1, sl, :] = ni
                        state[b][2 * cb], state[b][2 * cb + 1] = nr, ni
            return run
        return [make(t0) for t0 in range(0, tc, group)]

    def back_units(b):
        units = []
        for cb in range(ncb):
            def gather(cb=cb):
                for part in range(2):
                    for k in range(nslab):
                        c0 = part * ns + k * LANES
                        hc_scr[b, cb % 2, :, c0:c0 + LANES] = (
                            s_scr[b][cb * 2 + part, k * pitch:k * pitch + tc, :].astype(BF16))
            units.append(gather)

            def out_proj(cb=cb):
                cols = slice(cb * cw, (cb + 1) * cw)
                y = _dot(hc_scr[b, cb % 2], wc_ref[cb]) + d_ref[:, cols] * u_scr[b, :, cols]
                z_scr[b, :, cols] = _gelu(y).astype(BF16)
            units.append(out_proj)
        for c0 in range(0, d, nw):
            def glu(c0=c0):
                val = _dot(z_scr[b], wglu_ref[:, c0:c0 + nw])
                gate = _dot(z_scr[b], wglu_ref[:, d + c0:d + c0 + nw])
                o_ref[b, :, c0:c0 + nw] = x_ref[b, :, c0:c0 + nw] + val * _sigmoid(gate)
            units.append(glu)
        return units

    for slot in range(nbatch + 2):
        mxu = []
        if slot < nbatch:
            mxu += front_units(slot)
        if 0 <= slot - 2 < nbatch:
            mxu += back_units(slot - 2)
        vpu = scan_units(slot - 1) if 0 <= slot - 1 < nbatch else []
        for unit in _interleave(mxu, vpu):
            unit()
    for b in range(nbatch):
        for i in range(2 * ncb):
            h_scr[b * 2 * ncb + i] = state[b][i]


def _s5(x, g, w_in, wb, a_tiles, wc, d_skip, w_glu, *, casts=(), tc=256):
    nbatch, l, d = x.shape
    e = d_skip.shape[1]
    ncb = wb.shape[0]
    tc = min(tc, l)
    pitch = _s5_pitch(tc)
    nslab = S5_COLBLOCK_GROUPS * S5_STATE // LANES
    return _launch(
        functools.partial(_s5_kernel, pitch=pitch),
        steps=l // tc,
        inputs=(x, g, w_in, wb, a_tiles, wc, d_skip, w_glu),
        in_specs=[
            pl.BlockSpec((nbatch, tc, d), lambda i: (0, i, 0)),
            _resident((1, d)),
            _resident(w_in.shape),
            _resident(wb.shape),
            _resident(a_tiles.shape),
            _resident(wc.shape),
            _resident((1, e)),
            _resident(w_glu.shape),
        ],
        out_spec=pl.BlockSpec((nbatch, tc, d), lambda i: (0, i, 0)),
        out_shape=jax.ShapeDtypeStruct((nbatch, l, d), F32),
        scratch=[
            pltpu.VMEM((nbatch, tc, d), BF16),
            pltpu.VMEM((nbatch, tc, e), F32),
            pltpu.VMEM((nbatch * ncb * 2, SUBLANES, LANES), F32),
            pltpu.VMEM((nbatch, 2, tc, 2 * nslab * LANES), BF16),
            pltpu.VMEM((nbatch, tc, e), BF16),
        ] + [pltpu.VMEM((ncb * 2, nslab * pitch, LANES), F32) for _ in range(nbatch)],
        semantics="arbitrary",
        name="s5_mixer",
        casts=casts,
    )


def _s5_prepare(log_dt, a_re, a_im, b_re, b_im, c_re, c_im):
    ngroups, nstate = a_re.shape
    m = b_re.shape[2]
    dt = jnp.exp(log_dt.astype(F32))[:, None]
    ar, ai = a_re.astype(F32), a_im.astype(F32)
    mag = jnp.exp(ar * dt)
    ang = ai * dt
    abar_r, abar_i = mag * jnp.cos(ang), mag * jnp.sin(ang)
    den = ar * ar + ai * ai
    nr, ni = abar_r - 1.0, abar_i
    cr = (nr * ar + ni * ai) / den
    ci = (ni * ar - nr * ai) / den
    br, bi = b_re.astype(F32), b_im.astype(F32)
    bbar_r = cr[..., None] * br - ci[..., None] * bi
    bbar_i = cr[..., None] * bi + ci[..., None] * br
    gpb = S5_COLBLOCK_GROUPS
    ncb = ngroups // gpb

    def blockdiag(w):
        _, na, nb = w.shape
        wt = w.reshape(ncb, gpb, na, nb).transpose(0, 1, 3, 2).reshape(ncb, gpb * nb, na)
        r = lax.broadcasted_iota(jnp.int32, (gpb * nb, gpb * na), 0) // nb
        c = lax.broadcasted_iota(jnp.int32, (gpb * nb, gpb * na), 1) // na
        return jnp.where(r == c, jnp.tile(wt, (1, 1, gpb)), 0.0)

    wb = jnp.concatenate([blockdiag(bbar_r), blockdiag(bbar_i)], axis=2).astype(BF16)
    wc = jnp.concatenate([blockdiag(c_re.astype(F32)),
                          blockdiag(-c_im.astype(F32))], axis=1).astype(BF16)
    a_tiles = jnp.stack([abar_r, abar_i]).reshape(2, ncb, SUBLANES, LANES)
    return wb, a_tiles, wc


def _ret_kernel(x_ref, g_ref, win_ref, cc_ref, sc_ref, ct_ref, st_ref, gng_ref, wout_ref, o_ref,
                hn_scr, r_scr, gated_scr, dmask_scr, xi_scr, zeta_scr):
    nbatch, tc, d = x_ref.shape
    nheads = RET_HEADS
    dk = r_scr.shape[1]
    dv = r_scr.shape[2]
    qd = nheads * dk
    vd = nheads * dv
    half = dk // 2
    log_g = [math.log(1.0 - 2.0 ** (-5.0 - h)) for h in range(nheads)]

    @pl.when(pl.program_id(0) == 0)
    def _():
        r_scr[...] = jnp.zeros_like(r_scr)
        row = lax.broadcasted_iota(jnp.int32, (tc, tc), 0)
        col = lax.broadcasted_iota(jnp.int32, (tc, tc), 1)
        diff = (row - col).astype(F32)
        t = lax.broadcasted_iota(jnp.int32, (tc, LANES), 0).astype(F32)
        for h in range(nheads):
            dmask_scr[h] = jnp.where(diff >= 0, jnp.exp(jnp.maximum(diff, 0.0) * log_g[h]), 0.0)
            xi_scr[h] = jnp.exp((t + 1.0) * log_g[h])
            zeta_scr[h] = jnp.exp((tc - 1.0 - t) * log_g[h])

    for b in range(nbatch):
        hn_scr[b * tc:(b + 1) * tc, :] = _rms(x_ref[b], g_ref[...]).astype(BF16)

    cc, sc, ct, st = cc_ref[...], sc_ref[...], ct_ref[...], st_ref[...]
    cos = jnp.concatenate([cc * ct - sc * st] * nbatch, axis=0)
    sin = jnp.concatenate([sc * ct + cc * st] * nbatch, axis=0)

    def rotate(t):
        t1, t2 = t[:, :half], t[:, half:]
        return jnp.concatenate([t1 * cos - t2 * sin, t1 * sin + t2 * cos], axis=1)

    for h in range(nheads):
        dmask = dmask_scr[h]
        xi = jnp.concatenate([xi_scr[h]] * (dv // LANES), axis=1)
        zeta = jnp.concatenate([zeta_scr[h]] * (dk // LANES), axis=1)
        hn = hn_scr[...]
        q = rotate(_dot(hn, win_ref[:, h * dk:(h + 1) * dk])).astype(BF16)
        k = rotate(_dot(hn, win_ref[:, qd + h * dk:qd + (h + 1) * dk])) * (dk ** -0.5)
        v = _dot(hn, win_ref[:, 2 * qd + h * dv:2 * qd + (h + 1) * dv]).astype(BF16)
        gate = _dot(hn, win_ref[:, 2 * qd + vd + h * dv:2 * qd + vd + (h + 1) * dv])
        for b in range(nbatch):
            rows = slice(b * tc, (b + 1) * tc)
            kb = k[rows]
            s = lax.dot_general(q[rows], kb.astype(BF16), (((1,), (1,)), ((), ())),
                                preferred_element_type=F32) * dmask
            r_old = r_scr[b * nheads + h]
            o = _dot(s.astype(BF16), v[rows]) + _dot(q[rows], r_old.astype(BF16)) * xi
            r_scr[b * nheads + h] = math.exp(tc * log_g[h]) * r_old + lax.dot_general(
                (kb * zeta).astype(BF16), v[rows], (((0,), (0,)), ((), ())),
                preferred_element_type=F32)
            mu = jnp.mean(o, axis=-1, keepdims=True)
            do = o - mu
            var = jnp.mean(do * do, axis=-1, keepdims=True)
            on = do * lax.rsqrt(var + LN_EPS) * gng_ref[:, h * dv:(h + 1) * dv]
            gb = gate[rows]
            gated_scr[rows, h * dv:(h + 1) * dv] = (gb * _sigmoid(gb) * on).astype(BF16)
    nw = 512
    for c0 in range(0, d, nw):
        res = _dot(gated_scr[...], wout_ref[:, c0:c0 + nw])
        for b in range(nbatch):
            o_ref[b, :, c0:c0 + nw] = x_ref[b, :, c0:c0 + nw] + res[b * tc:(b + 1) * tc]


def _retention(x, g, w_in, gn_g, w_out, *, casts=(), tc=256):
    nbatch, l, d = x.shape
    tc = min(tc, l)
    dk = d // RET_HEADS
    dv = 2 * dk
    vd = RET_HEADS * dv
    inv_freq = ROPE_BASE ** (-jnp.arange(0, dk, 2, dtype=F32) / dk)
    th_c = (jnp.arange(l // tc, dtype=F32) * tc)[:, None, None] * inv_freq
    th_t = jnp.arange(tc, dtype=F32)[:, None] * inv_freq
    chunk_spec = pl.BlockSpec((None, 1, dk // 2), lambda i: (i, 0, 0))
    return _launch(
        _ret_kernel,
        steps=l // tc,
        inputs=(x, g, w_in, jnp.cos(th_c), jnp.sin(th_c), jnp.cos(th_t), jnp.sin(th_t), gn_g, w_out),
        in_specs=[
            pl.BlockSpec((nbatch, tc, d), lambda i: (0, i, 0)),
            _resident((1, d)),
            _resident(w_in.shape),
            chunk_spec,
            chunk_spec,
            _resident((tc, dk // 2)),
            _resident((tc, dk // 2)),
            _resident((1, vd)),
            _resident(w_out.shape),
        ],
        out_spec=pl.BlockSpec((nbatch, tc, d), lambda i: (0, i, 0)),
        out_shape=jax.ShapeDtypeStruct((nbatch, l, d), F32),
        scratch=[pltpu.VMEM((nbatch * tc, d), BF16),
                 pltpu.VMEM((nbatch * RET_HEADS, dk, dv), F32),
                 pltpu.VMEM((nbatch * tc, vd), BF16),
                 pltpu.VMEM((RET_HEADS, tc, tc), F32),
                 pltpu.VMEM((RET_HEADS, tc, LANES), F32),
                 pltpu.VMEM((RET_HEADS, tc, LANES), F32)],
        semantics="arbitrary",
        name="retention_mixer",
        casts=casts,
    )


def _sconv_kernel(x_ref, g_ref, win_ref, cw_ref, wout_ref, o_ref, hn_scr, zs_scr, gy_scr):
    nbatch, tc, d = x_ref.shape
    e = cw_ref.shape[1]
    pad = SUBLANES

    @pl.when(pl.program_id(0) == 0)
    def _():
        zs_scr[:, 0:pad, :] = jnp.zeros((nbatch, pad, e), F32)

    nw = 512
    for b in range(nbatch):
        rows = slice(b * tc, (b + 1) * tc)
        hn_scr[rows, :] = _rms(x_ref[b], g_ref[...]).astype(BF16)
        for c0 in range(0, e, nw):
            cols = slice(c0, c0 + nw)
            hn = hn_scr[rows, :]
            gate_b = _dot(hn, win_ref[:, c0:c0 + nw])
            z = (_dot(hn, win_ref[:, e + c0:e + c0 + nw])
                 * _dot(hn, win_ref[:, 2 * e + c0:2 * e + c0 + nw]))
            zs_scr[b, pad:pad + tc, cols] = z
            y = (cw_ref[0:1, cols] * zs_scr[b, pad - 2:pad - 2 + tc, cols]
                 + cw_ref[1:2, cols] * zs_scr[b, pad - 1:pad - 1 + tc, cols]
                 + cw_ref[2:3, cols] * z)
            zs_scr[b, 0:pad, cols] = zs_scr[b, tc:tc + pad, cols]
            gy_scr[rows, cols] = (gate_b * y).astype(BF16)
    for b in range(nbatch):
        rows = slice(b * tc, (b + 1) * tc)
        for c0 in range(0, d, nw):
            o_ref[b, :, c0:c0 + nw] = (x_ref[b, :, c0:c0 + nw]
                                       + _dot(gy_scr[rows, :], wout_ref[:, c0:c0 + nw]))


def _sconv(x, g, w_in, conv_w, w_out, *, casts=(), tc=256):
    nbatch, l, d = x.shape
    e = conv_w.shape[1]
    tc = min(tc, l)
    return _launch(
        _sconv_kernel,
        steps=l // tc,
        inputs=(x, g, w_in, conv_w, w_out),
        in_specs=[
            pl.BlockSpec((nbatch, tc, d), lambda i: (0, i, 0)),
            _resident((1, d)),
            _resident(w_in.shape),
            _resident(conv_w.shape),
            _resident(w_out.shape),
        ],
        out_spec=pl.BlockSpec((nbatch, tc, d), lambda i: (0, i, 0)),
        out_shape=jax.ShapeDtypeStruct((nbatch, l, d), F32),
        scratch=[pltpu.VMEM((nbatch * tc, d), BF16),
                 pltpu.VMEM((nbatch, tc + SUBLANES, e), F32),
                 pltpu.VMEM((nbatch * tc, e), BF16)],
        semantics="arbitrary",
        name="sconv_mixer",
        casts=casts,
    )


def kernel(x, norm1_g, norm2_g, mlp_w1, mlp_w2, final_g, a_w_in, a_ln_g, a_ws, a_bs, a_w_out, b_w_in, b_log_dt, b_a_re, b_a_im, b_b_re, b_b_im, b_c_re, b_c_im, b_d, b_w_glu, c_w_in, c_gn_g, c_w_out, d_w_in, d_conv_w, d_w_out):
    nbatch, l, d = x.shape
    depth = norm1_g.shape[0]
    n_mixers = 4
    row = lambda v: v.reshape(1, -1).astype(F32)

    def mixer_weights(i):
        m, j = i % n_mixers, i // n_mixers
        pairs = ((a_w_in, a_w_out), (b_w_in, b_w_glu), (c_w_in, c_w_out), (d_w_in, d_w_out))[m]
        return [(w, j) for w in pairs]

    w_in, w_out = (w[j].astype(BF16) for w, j in mixer_weights(0))
    for i in range(depth):
        m, j = i % n_mixers, i // n_mixers
        g1 = row(norm1_g[i])
        mlp_casts = [(mlp_w1, i), (mlp_w2, i)]
        if m == 0:
            gd = a_ln_g.shape[1] // GM_GROUPS
            bs_b = jnp.broadcast_to(a_bs[j].astype(F32)[:, :, None], (GM_GROUPS, CHUNK, gd))
            x, (w1, w2) = _gmlp(x.reshape(nbatch * l, d), g1, w_in, row(a_ln_g[j]),
                                a_ws[j].astype(F32), bs_b, w_out, casts=mlp_casts)
        elif m == 1:
            wb, a_tiles, wc = _s5_prepare(b_log_dt[j], b_a_re[j], b_a_im[j], b_b_re[j],
                                          b_b_im[j], b_c_re[j], b_c_im[j])
            x, (w1, w2) = _s5(x, g1, w_in, wb, a_tiles, wc, row(b_d[j]), w_out, casts=mlp_casts)
        elif m == 2:
            x, (w1, w2) = _retention(x, g1, w_in, row(c_gn_g[j]), w_out, casts=mlp_casts)
        else:
            x, (w1, w2) = _sconv(x, g1, w_in, d_conv_w[j].reshape(CONV_WIDTH, -1).astype(F32),
                                 w_out, casts=mlp_casts)
        last = i == depth - 1
        x, nxt = _mlp(x.reshape(nbatch * l, d), row(norm2_g[i]), w1, w2, row(final_g),
                      final_norm=last, casts=() if last else mixer_weights(i + 1))
        x = x.reshape(nbatch, l, d)
        if not last:
            w_in, w_out = nxt
    return x
```

```python
import functools
import math

import jax
import jax.numpy as jnp
from jax import lax
from jax.experimental import pallas as pl
from jax.experimental.pallas import tpu as pltpu

F32 = jnp.float32
BF16 = jnp.bfloat16

EPS = 1e-6
LN_EPS = 1e-5
CHUNK = 128
GM_GROUPS = 8
S5_GROUP = 16
S5_STATE = 64
S5_COLBLOCK_GROUPS = 16
RET_HEADS = 4
ROPE_BASE = 10000.0
CONV_WIDTH = 3

LANES = 128
SUBLANES = 8
VMEM_LIMIT = 56 * 1024 * 1024


def _dot(a, b):
    return jnp.dot(a, b, preferred_element_type=F32)


def _rms(x, g):
    return x * lax.rsqrt(jnp.mean(x * x, axis=-1, keepdims=True) + EPS) * g


def _gelu(x):
    c = math.sqrt(2.0 / math.pi)
    return x * (0.5 + 0.5 * jnp.tanh(x * (c + (c * 0.044715) * (x * x))))


def _sigmoid(x):
    return 1.0 / (1.0 + jnp.exp(-x))


def _interleave(xs, ys):
    if not xs:
        return list(ys)
    out, j = [], 0
    for i, unit in enumerate(xs):
        out.append(unit)
        upto = (i + 1) * len(ys) // len(xs)
        out += ys[j:upto]
        j = upto
    return out + ys[j:]


def _resident(shape):
    nd = len(shape)
    return pl.BlockSpec(shape, lambda *_: (0,) * nd, pipeline_mode=pl.Buffered(1))


def _params(semantics):
    return pltpu.CompilerParams(dimension_semantics=semantics, vmem_limit_bytes=VMEM_LIMIT)


def _launch(body, *, steps, inputs, in_specs, out_spec, out_shape, scratch, semantics, name,
            casts=()):
    n_in, ncast = len(inputs), len(casts)

    def kern(*refs):
        cast_in = refs[n_in:n_in + ncast]
        cast_out = refs[n_in + ncast + 1:n_in + 2 * ncast + 1]
        for src, dst in zip(cast_in, cast_out):
            dst[...] = src[...].astype(BF16)
        body(*refs[:n_in], refs[n_in + ncast], *refs[n_in + 2 * ncast + 1:])

    cast_in_specs, cast_out_specs, cast_shapes = [], [], []
    for w, layer in casts:
        _, rows, cols = w.shape
        rb = rows // steps
        assert rb * steps == rows and rb % (2 * SUBLANES) == 0, (w.shape, steps)
        cast_in_specs.append(pl.BlockSpec((None, rb, cols), lambda i, layer=layer: (layer, i, 0)))
        cast_out_specs.append(pl.BlockSpec((rb, cols), lambda i: (i, 0)))
        cast_shapes.append(jax.ShapeDtypeStruct((rows, cols), BF16))
    res = pl.pallas_call(
        kern,
        grid=(steps,),
        in_specs=list(in_specs) + cast_in_specs,
        out_specs=[out_spec] + cast_out_specs,
        out_shape=[out_shape] + cast_shapes,
        scratch_shapes=scratch,
        compiler_params=_params((semantics,)),
        name=name,
    )(*inputs, *[w for w, _ in casts])
    return res[0], list(res[1:])


def _mlp_kernel(x_ref, g_ref, w1_ref, w2_ref, fg_ref, o_ref, hn_scr, a_scr, *,
                final_norm, nf, tr):
    dff = w1_ref.shape[1]
    for r0 in range(0, x_ref.shape[0], tr):
        rows = slice(r0, r0 + tr)
        hn_scr[rows, :] = _rms(x_ref[rows, :], g_ref[...]).astype(BF16)
        for n0 in range(0, dff, nf):
            a = jnp.maximum(_dot(hn_scr[rows, :], w1_ref[:, n0:n0 + nf]), 0.0)
            a_scr[rows, n0:n0 + nf] = (a * a).astype(BF16)
    for r0 in range(0, x_ref.shape[0], tr):
        rows = slice(r0, r0 + tr)
        y = x_ref[rows, :] + _dot(a_scr[rows, :], w2_ref[...])
        if final_norm:
            y = _rms(y, fg_ref[...])
        o_ref[rows, :] = y


def _mlp(x2, g, w1, w2, final_g, *, final_norm, casts=(), tm=1024, tr=512, nf=256):
    n, d = x2.shape
    dff = w1.shape[1]
    tm = min(tm, n)
    return _launch(
        functools.partial(_mlp_kernel, final_norm=final_norm, nf=nf, tr=min(tr, tm)),
        steps=n // tm,
        inputs=(x2, g, w1, w2, final_g),
        in_specs=[
            pl.BlockSpec((tm, d), lambda i: (i, 0)),
            _resident((1, d)),
            _resident(w1.shape),
            _resident(w2.shape),
            _resident((1, d)),
        ],
        out_spec=pl.BlockSpec((tm, d), lambda i: (i, 0)),
        out_shape=jax.ShapeDtypeStruct((n, d), F32),
        scratch=[pltpu.VMEM((tm, d), BF16), pltpu.VMEM((tm, dff), BF16)],
        semantics="parallel",
        name="mlp_final" if final_norm else "mlp",
        casts=casts,
    )


def _gmlp_kernel(x_ref, g_ref, win_ref, lng_ref, ws_ref, bs_ref, wout_ref, o_ref,
                 hn_scr, u_scr, v_scr, vb_scr, gated_scr, *, tr):
    tc = x_ref.shape[0]
    e = lng_ref.shape[1]
    n_groups = ws_ref.shape[0]
    gd = e // n_groups
    nb = 256
    row = lax.broadcasted_iota(jnp.int32, (CHUNK, CHUNK), 0)
    col = lax.broadcasted_iota(jnp.int32, (CHUNK, CHUNK), 1)
    ws = [jnp.where(row >= col, ws_ref[g], 0.0).astype(BF16) for g in range(n_groups)]

    def stages(r0):
        rows = slice(r0, r0 + tr)

        def norm():
            hn_scr[rows, :] = _rms(x_ref[rows, :], g_ref[...]).astype(BF16)

        def v_unit(c0):
            v_scr[rows, c0:c0 + nb] = _gelu(_dot(hn_scr[rows, :], win_ref[:, e + c0:e + c0 + nb]))

        def ln_unit(q0):
            v = v_scr[q0:q0 + CHUNK, :]
            mu = jnp.mean(v, axis=-1, keepdims=True)
            dv = v - mu
            var = jnp.mean(dv * dv, axis=-1, keepdims=True)
            vb_scr[q0:q0 + CHUNK, :] = (dv * lax.rsqrt(var + LN_EPS) * lng_ref[...]).astype(BF16)

        def u_unit(c0):
            u_scr[rows, c0:c0 + nb] = _gelu(_dot(hn_scr[rows, :], win_ref[:, c0:c0 + nb]))

        def mix_unit(g, q0):
            q, cols = slice(q0, q0 + CHUNK), slice(g * gd, (g + 1) * gd)
            sv = _dot(ws[g], vb_scr[q, cols]) + bs_ref[g]
            gated_scr[q, cols] = (u_scr[q, cols] * sv).astype(BF16)

        def out_unit(c0):
            o_ref[rows, c0:c0 + nb] = (x_ref[rows, c0:c0 + nb]
                                       + _dot(gated_scr[rows, :], wout_ref[:, c0:c0 + nb]))

        part = functools.partial
        chunks = range(r0, r0 + tr, CHUNK)
        return [
            [norm] + [part(v_unit, c0) for c0 in range(0, e, nb)],
            [part(ln_unit, q0) for q0 in chunks],
            [part(u_unit, c0) for c0 in range(0, e, nb)],
            [part(mix_unit, g, q0) for g in range(n_groups) for q0 in chunks],
            [part(out_unit, c0) for c0 in range(0, x_ref.shape[1], nb)],
        ]

    pipes = [stages(r0) for r0 in range(0, tc, tr)]
    nstage = len(pipes[0])
    for slot in range(nstage + len(pipes) - 1):
        units = []
        for p, pipe in enumerate(pipes):
            if 0 <= slot - p < nstage:
                units = _interleave(units, pipe[slot - p]) if units else list(pipe[slot - p])
        for unit in units:
            unit()


def _gmlp(x2, g, w_in, ln_g, ws, bs_b, w_out, *, casts=(), tc=512, tr=256):
    n, d = x2.shape
    e = ln_g.shape[1]
    tc = min(tc, n)
    return _launch(
        functools.partial(_gmlp_kernel, tr=min(tr, tc)),
        steps=n // tc,
        inputs=(x2, g, w_in, ln_g, ws, bs_b, w_out),
        in_specs=[
            pl.BlockSpec((tc, d), lambda i: (i, 0)),
            _resident((1, d)),
            _resident(w_in.shape),
            _resident((1, e)),
            _resident(ws.shape),
            _resident(bs_b.shape),
            _resident(w_out.shape),
        ],
        out_spec=pl.BlockSpec((tc, d), lambda i: (i, 0)),
        out_shape=jax.ShapeDtypeStruct((n, d), F32),
        scratch=[pltpu.VMEM((tc, d), BF16),
                 pltpu.VMEM((tc, e), F32), pltpu.VMEM((tc, e), F32),
                 pltpu.VMEM((tc, e), BF16), pltpu.VMEM((tc, e), BF16)],
        semantics="parallel",
        name="gmlp_mixer",
        casts=casts,
    )


def _s5_pitch(tc):
    assert tc % SUBLANES == 0
    return tc + SUBLANES // 2


def _s5_kernel(x_ref, g_ref, win_ref, wb_ref, a_ref, wc_ref, d_ref, wglu_ref, o_ref,
               hn_scr, u_scr, h_scr, hc_scr, z_scr, *s_scr, pitch):
    nbatch, tc, d = x_ref.shape
    e = d_ref.shape[1]
    ncb = wb_ref.shape[0]
    cw = S5_COLBLOCK_GROUPS * S5_GROUP
    ns = S5_COLBLOCK_GROUPS * S5_STATE
    nslab = ns // LANES

    @pl.when(pl.program_id(0) == 0)
    def _():
        h_scr[...] = jnp.zeros_like(h_scr)

    nw = 2 * LANES
    ar = [a_ref[0, cb] for cb in range(ncb)]
    ai = [a_ref[1, cb] for cb in range(ncb)]
    state = [[h_scr[b * 2 * ncb + i] for i in range(2 * ncb)] for b in range(nbatch)]


    def front_units(b):
        units = []

        def norm():
            hn_scr[b] = _rms(x_ref[b], g_ref[...]).astype(BF16)
        units.append(norm)
        for c0 in range(0, e, 2 * nw):
            def proj(c0=c0):
                u_scr[b, :, c0:c0 + 2 * nw] = _dot(hn_scr[b], win_ref[:, c0:c0 + 2 * nw])
            units.append(proj)
        for cb in range(ncb):
            for c0 in range(0, 2 * ns, nw):
                def bu_unit(cb=cb, c0=c0):
                    ub = u_scr[b, :, cb * cw:(cb + 1) * cw].astype(BF16)
                    bu = _dot(ub, wb_ref[cb, :, c0:c0 + nw])
                    part, k0 = c0 // ns, (c0 % ns) // LANES
                    for kk in range(nw // LANES):
                        r0 = (k0 + kk) * pitch
                        s_scr[b][cb * 2 + part, r0:r0 + tc, :] = bu[:, kk * LANES:(kk + 1) * LANES]
                units.append(bu_unit)
        return units

    def scan_units(b, group=8):
        def make(t0):
            def run():
                for t in range(t0, t0 + group):
                    sl = pl.ds(t, SUBLANES, stride=pitch)
                    for cb in range(ncb):
                        hr, hi = state[b][2 * cb], state[b][2 * cb + 1]
                        nr = ar[cb] * hr - ai[cb] * hi + s_scr[b][2 * cb, sl, :]
                        ni = ar[cb] * hi + ai[cb] * hr + s_scr[b][2 * cb + 1, sl, :]
                        s_scr[b][2 * cb, sl, :] = nr
                        s_scr[b][2 * cb + 1, sl, :] = ni
                        state[b][2 * cb], state[b][2 * cb + 1] = nr, ni
            return run
        return [make(t0) for t0 in range(0, tc, group)]

    def back_units(b):
        units = []
        for cb in range(ncb):
            def gather(cb=cb):
                for part in range(2):
                    for k in range(nslab):
                        c0 = part * ns + k * LANES
                        hc_scr[b, cb % 2, :, c0:c0 + LANES] = (
                            s_scr[b][cb * 2 + part, k * pitch:k * pitch + tc, :].astype(BF16))
            units.append(gather)

            def out_proj(cb=cb):
                cols = slice(cb * cw, (cb + 1) * cw)
                y = _dot(hc_scr[b, cb % 2], wc_ref[cb]) + d_ref[:, cols] * u_scr[b, :, cols]
                z_scr[b, :, cols] = _gelu(y).astype(BF16)
            units.append(out_proj)
        for c0 in range(0, d, nw):
            def glu(c0=c0):
                val = _dot(z_scr[b], wglu_ref[:, c0:c0 + nw])
                gate = _dot(z_scr[b], wglu_ref[:, d + c0:d + c0 + nw])
                o_ref[b, :, c0:c0 + nw] = x_ref[b, :, c0:c0 + nw] + val * _sigmoid(gate)
            units.append(glu)
        return units

    for slot in range(nbatch + 2):
        mxu = []
        if slot < nbatch:
            mxu += front_units(slot)
        if 0 <= slot - 2 < nbatch:
            mxu += back_units(slot - 2)
        vpu = scan_units(slot - 1) if 0 <= slot - 1 < nbatch else []
        for unit in _interleave(mxu, vpu):
            unit()
    for b in range(nbatch):
        for i in range(2 * ncb):
            h_scr[b * 2 * ncb + i] = state[b][i]


def _s5(x, g, w_in, wb, a_tiles, wc, d_skip, w_glu, *, casts=(), tc=256):
    nbatch, l, d = x.shape
    e = d_skip.shape[1]
    ncb = wb.shape[0]
    tc = min(tc, l)
    pitch = _s5_pitch(tc)
    nslab = S5_COLBLOCK_GROUPS * S5_STATE // LANES
    return _launch(
        functools.partial(_s5_kernel, pitch=pitch),
        steps=l // tc,
        inputs=(x, g, w_in, wb, a_tiles, wc, d_skip, w_glu),
        in_specs=[
            pl.BlockSpec((nbatch, tc, d), lambda i: (0, i, 0)),
            _resident((1, d)),
            _resident(w_in.shape),
            _resident(wb.shape),
            _resident(a_tiles.shape),
            _resident(wc.shape),
            _resident((1, e)),
            _resident(w_glu.shape),
        ],
        out_spec=pl.BlockSpec((nbatch, tc, d), lambda i: (0, i, 0)),
        out_shape=jax.ShapeDtypeStruct((nbatch, l, d), F32),
        scratch=[
            pltpu.VMEM((nbatch, tc, d), BF16),
            pltpu.VMEM((nbatch, tc, e), F32),
            pltpu.VMEM((nbatch * ncb * 2, SUBLANES, LANES), F32),
            pltpu.VMEM((nbatch, 2, tc, 2 * nslab * LANES), BF16),
            pltpu.VMEM((nbatch, tc, e), BF16),
        ] + [pltpu.VMEM((ncb * 2, nslab * pitch, LANES), F32) for _ in range(nbatch)],
        semantics="arbitrary",
        name="s5_mixer",
        casts=casts,
    )


def _s5_prepare(log_dt, a_re, a_im, b_re, b_im, c_re, c_im):
    ngroups, nstate = a_re.shape
    m = b_re.shape[2]
    dt = jnp.exp(log_dt.astype(F32))[:, None]
    ar, ai = a_re.astype(F32), a_im.astype(F32)
    mag = jnp.exp(ar * dt)
    ang = ai * dt
    abar_r, abar_i = mag * jnp.cos(ang), mag * jnp.sin(ang)
    den = ar * ar + ai * ai
    nr, ni = abar_r - 1.0, abar_i
    cr = (nr * ar + ni * ai) / den
    ci = (ni * ar - nr * ai) / den
    br, bi = b_re.astype(F32), b_im.astype(F32)
    bbar_r = cr[..., None] * br - ci[..., None] * bi
    bbar_i = cr[..., None] * bi + ci[..., None] * br
    gpb = S5_COLBLOCK_GROUPS
    ncb = ngroups // gpb

    def blockdiag(w):
        _, na, nb = w.shape
        wt = w.reshape(ncb, gpb, na, nb).transpose(0, 1, 3, 2).reshape(ncb, gpb * nb, na)
        r = lax.broadcasted_iota(jnp.int32, (gpb * nb, gpb * na), 0) // nb
        c = lax.broadcasted_iota(jnp.int32, (gpb * nb, gpb * na), 1) // na
        return jnp.where(r == c, jnp.tile(wt, (1, 1, gpb)), 0.0)

    wb = jnp.concatenate([blockdiag(bbar_r), blockdiag(bbar_i)], axis=2).astype(BF16)
    wc = jnp.concatenate([blockdiag(c_re.astype(F32)),
                          blockdiag(-c_im.astype(F32))], axis=1).astype(BF16)
    a_tiles = jnp.stack([abar_r, abar_i]).reshape(2, ncb, SUBLANES, LANES)
    return wb, a_tiles, wc


def _ret_kernel(x_ref, g_ref, win_ref, cc_ref, sc_ref, ct_ref, st_ref, gng_ref, wout_ref, o_ref,
                hn_scr, r_scr, gated_scr, dmask_scr, xi_scr, zeta_scr):
    nbatch, tc, d = x_ref.shape
    nheads = RET_HEADS
    dk = r_scr.shape[1]
    dv = r_scr.shape[2]
    qd = nheads * dk
    vd = nheads * dv
    half = dk // 2
    log_g = [math.log(1.0 - 2.0 ** (-5.0 - h)) for h in range(nheads)]

    @pl.when(pl.program_id(0) == 0)
    def _():
        r_scr[...] = jnp.zeros_like(r_scr)
        row = lax.broadcasted_iota(jnp.int32, (tc, tc), 0)
        col = lax.broadcasted_iota(jnp.int32, (tc, tc), 1)
        diff = (row - col).astype(F32)
        t = lax.broadcasted_iota(jnp.int32, (tc, LANES), 0).astype(F32)
        for h in range(nheads):
            dmask_scr[h] = jnp.where(diff >= 0, jnp.exp(jnp.maximum(diff, 0.0) * log_g[h]), 0.0)
            xi_scr[h] = jnp.exp((t + 1.0) * log_g[h])
            zeta_scr[h] = jnp.exp((tc - 1.0 - t) * log_g[h])

    for b in range(nbatch):
        hn_scr[b * tc:(b + 1) * tc, :] = _rms(x_ref[b], g_ref[...]).astype(BF16)

    cc, sc, ct, st = cc_ref[...], sc_ref[...], ct_ref[...], st_ref[...]
    cos = jnp.concatenate([cc * ct - sc * st] * nbatch, axis=0)
    sin = jnp.concatenate([sc * ct + cc * st] * nbatch, axis=0)

    def rotate(t):
        t1, t2 = t[:, :half], t[:, half:]
        return jnp.concatenate([t1 * cos - t2 * sin, t1 * sin + t2 * cos], axis=1)

    for h in range(nheads):
        dmask = dmask_scr[h]
        xi = jnp.concatenate([xi_scr[h]] * (dv // LANES), axis=1)
        zeta = jnp.concatenate([zeta_scr[h]] * (dk // LANES), axis=1)
        hn = hn_scr[...]
        q = rotate(_dot(hn, win_ref[:, h * dk:(h + 1) * dk])).astype(BF16)
        k = rotate(_dot(hn, win_ref[:, qd + h * dk:qd + (h + 1) * dk])) * (dk ** -0.5)
        v = _dot(hn, win_ref[:, 2 * qd + h * dv:2 * qd + (h + 1) * dv]).astype(BF16)
        gate = _dot(hn, win_ref[:, 2 * qd + vd + h * dv:2 * qd + vd + (h + 1) * dv])
        for b in range(nbatch):
            rows = slice(b * tc, (b + 1) * tc)
            kb = k[rows]
            s = lax.dot_general(q[rows], kb.astype(BF16), (((1,), (1,)), ((), ())),
                                preferred_element_type=F32) * dmask
            r_old = r_scr[b * nheads + h]
            o = _dot(s.astype(BF16), v[rows]) + _dot(q[rows], r_old.astype(BF16)) * xi
            r_scr[b * nheads + h] = math.exp(tc * log_g[h]) * r_old + lax.dot_general(
                (kb * zeta).astype(BF16), v[rows], (((0,), (0,)), ((), ())),
                preferred_element_type=F32)
            mu = jnp.mean(o, axis=-1, keepdims=True)
            do = o - mu
            var = jnp.mean(do * do, axis=-1, keepdims=True)
            on = do * lax.rsqrt(var + LN_EPS) * gng_ref[:, h * dv:(h + 1) * dv]
            gb = gate[rows]
            gated_scr[rows, h * dv:(h + 1) * dv] = (gb * _sigmoid(gb) * on).astype(BF16)
    nw = 512
    for c0 in range(0, d, nw):
        res = _dot(gated_scr[...], wout_ref[:, c0:c0 + nw])
        for b in range(nbatch):
            o_ref[b, :, c0:c0 + nw] = x_ref[b, :, c0:c0 + nw] + res[b * tc:(b + 1) * tc]


def _retention(x, g, w_in, gn_g, w_out, *, casts=(), tc=256):
    nbatch, l, d = x.shape
    tc = min(tc, l)
    dk = d // RET_HEADS
    dv = 2 * dk
    vd = RET_HEADS * dv
    inv_freq = ROPE_BASE ** (-jnp.arange(0, dk, 2, dtype=F32) / dk)
    th_c = (jnp.arange(l // tc, dtype=F32) * tc)[:, None, None] * inv_freq
    th_t = jnp.arange(tc, dtype=F32)[:, None] * inv_freq
    chunk_spec = pl.BlockSpec((None, 1, dk // 2), lambda i: (i, 0, 0))
    return _launch(
        _ret_kernel,
        steps=l // tc,
        inputs=(x, g, w_in, jnp.cos(th_c), jnp.sin(th_c), jnp.cos(th_t), jnp.sin(th_t), gn_g, w_out),
        in_specs=[
            pl.BlockSpec((nbatch, tc, d), lambda i: (0, i, 0)),
            _resident((1, d)),
            _resident(w_in.shape),
            chunk_spec,
            chunk_spec,
            _resident((tc, dk // 2)),
            _resident((tc, dk // 2)),
            _resident((1, vd)),
            _resident(w_out.shape),
        ],
        out_spec=pl.BlockSpec((nbatch, tc, d), lambda i: (0, i, 0)),
        out_shape=jax.ShapeDtypeStruct((nbatch, l, d), F32),
        scratch=[pltpu.VMEM((nbatch * tc, d), BF16),
                 pltpu.VMEM((nbatch * RET_HEADS, dk, dv), F32),
                 pltpu.VMEM((nbatch * tc, vd), BF16),
                 pltpu.VMEM((RET_HEADS, tc, tc), F32),
                 pltpu.VMEM((RET_HEADS, tc, LANES), F32),
                 pltpu.VMEM((RET_HEADS, tc, LANES), F32)],
        semantics="arbitrary",
        name="retention_mixer",
        casts=casts,
    )


def _sconv_kernel(x_ref, g_ref, win_ref, cw_ref, wout_ref, o_ref, hn_scr, zs_scr, gy_scr):
    nbatch, tc, d = x_ref.shape
    e = cw_ref.shape[1]
    pad = SUBLANES

    @pl.when(pl.program_id(0) == 0)
    def _():
        zs_scr[:, 0:pad, :] = jnp.zeros((nbatch, pad, e), F32)

    for b in range(nbatch):
        hn_scr[b * tc:(b + 1) * tc, :] = _rms(x_ref[b], g_ref[...]).astype(BF16)
    nw = 512
    for c0 in range(0, e, nw):
        cols = slice(c0, c0 + nw)
        hn = hn_scr[...]
        gate_b = _dot(hn, win_ref[:, c0:c0 + nw])
        z = (_dot(hn, win_ref[:, e + c0:e + c0 + nw])
             * _dot(hn, win_ref[:, 2 * e + c0:2 * e + c0 + nw]))
        for b in range(nbatch):
            rows = slice(b * tc, (b + 1) * tc)
            zs_scr[b, pad:pad + tc, cols] = z[rows]
            y = (cw_ref[0:1, cols] * zs_scr[b, pad - 2:pad - 2 + tc, cols]
                 + cw_ref[1:2, cols] * zs_scr[b, pad - 1:pad - 1 + tc, cols]
                 + cw_ref[2:3, cols] * z[rows])
            zs_scr[b, 0:pad, cols] = zs_scr[b, tc:tc + pad, cols]
            gy_scr[rows, cols] = (gate_b[rows] * y).astype(BF16)
    for c0 in range(0, d, nw):
        res = _dot(gy_scr[...], wout_ref[:, c0:c0 + nw])
        for b in range(nbatch):
            o_ref[b, :, c0:c0 + nw] = x_ref[b, :, c0:c0 + nw] + res[b * tc:(b + 1) * tc]


def _sconv(x, g, w_in, conv_w, w_out, *, casts=(), tc=512):
    nbatch, l, d = x.shape
    e = conv_w.shape[1]
    tc = min(tc, l)
    return _launch(
        _sconv_kernel,
        steps=l // tc,
        inputs=(x, g, w_in, conv_w, w_out),
        in_specs=[
            pl.BlockSpec((nbatch, tc, d), lambda i: (0, i, 0)),
            _resident((1, d)),
            _resident(w_in.shape),
            _resident(conv_w.shape),
            _resident(w_out.shape),
        ],
        out_spec=pl.BlockSpec((nbatch, tc, d), lambda i: (0, i, 0)),
        out_shape=jax.ShapeDtypeStruct((nbatch, l, d), F32),
        scratch=[pltpu.VMEM((nbatch * tc, d), BF16),
                 pltpu.VMEM((nbatch, tc + SUBLANES, e), F32),
                 pltpu.VMEM((nbatch * tc, e), BF16)],
        semantics="arbitrary",
        name="sconv_mixer",
        casts=casts,
    )


def kernel(x, norm1_g, norm2_g, mlp_w1, mlp_w2, final_g, a_w_in, a_ln_g, a_ws, a_bs, a_w_out, b_w_in, b_log_dt, b_a_re, b_a_im, b_b_re, b_b_im, b_c_re, b_c_im, b_d, b_w_glu, c_w_in, c_gn_g, c_w_out, d_w_in, d_conv_w, d_w_out):
    nbatch, l, d = x.shape
    depth = norm1_g.shape[0]
    n_mixers = 4
    row = lambda v: v.reshape(1, -1).astype(F32)

    def mixer_weights(i):
        m, j = i % n_mixers, i // n_mixers
        pairs = ((a_w_in, a_w_out), (b_w_in, b_w_glu), (c_w_in, c_w_out), (d_w_in, d_w_out))[m]
        return [(w, j) for w in pairs]

    w_in, w_out = (w[j].astype(BF16) for w, j in mixer_weights(0))
    for i in range(depth):
        m, j = i % n_mixers, i // n_mixers
        g1 = row(norm1_g[i])
        mlp_casts = [(mlp_w1, i), (mlp_w2, i)]
        if m == 0:
            gd = a_ln_g.shape[1] // GM_GROUPS
            bs_b = jnp.broadcast_to(a_bs[j].astype(F32)[:, :, None], (GM_GROUPS, CHUNK, gd))
            x, (w1, w2) = _gmlp(x.reshape(nbatch * l, d), g1, w_in, row(a_ln_g[j]),
                                a_ws[j].astype(F32), bs_b, w_out, casts=mlp_casts)
        elif m == 1:
            wb, a_tiles, wc = _s5_prepare(b_log_dt[j], b_a_re[j], b_a_im[j], b_b_re[j],
                                          b_b_im[j], b_c_re[j], b_c_im[j])
            x, (w1, w2) = _s5(x, g1, w_in, wb, a_tiles, wc, row(b_d[j]), w_out, casts=mlp_casts)
        elif m == 2:
            x, (w1, w2) = _retention(x, g1, w_in, row(c_gn_g[j]), w_out, casts=mlp_casts)
        else:
            x, (w1, w2) = _sconv(x, g1, w_in, d_conv_w[j].reshape(CONV_WIDTH, -1).astype(F32),
                                 w_out, casts=mlp_casts)
        last = i == depth - 1
        x, nxt = _mlp(x.reshape(nbatch * l, d), row(norm2_g[i]), w1, w2, row(final_g),
                      final_norm=last, casts=() if last else mixer_weights(i + 1))
        x = x.reshape(nbatch, l, d)
        if not last:
            w_in, w_out = nxt
    return x
```

```python
import functools
import math

import jax
import jax.numpy as jnp
from jax import lax
from jax.experimental import pallas as pl
from jax.experimental.pallas import tpu as pltpu

F32 = jnp.float32
BF16 = jnp.bfloat16

EPS = 1e-6
LN_EPS = 1e-5
CHUNK = 128
GM_GROUPS = 8
S5_GROUP = 16
S5_STATE = 64
S5_COLBLOCK_GROUPS = 16
RET_HEADS = 4
ROPE_BASE = 10000.0
CONV_WIDTH = 3

LANES = 128
SUBLANES = 8
VMEM_LIMIT = 56 * 1024 * 1024


def _dot(a, b):
    return jnp.dot(a, b, preferred_element_type=F32)


def _rms(x, g):
    return x * lax.rsqrt(jnp.mean(x * x, axis=-1, keepdims=True) + EPS) * g


def _gelu(x):
    c = math.sqrt(2.0 / math.pi)
    return x * (0.5 + 0.5 * jnp.tanh(x * (c + (c * 0.044715) * (x * x))))


def _sigmoid(x):
    return 1.0 / (1.0 + jnp.exp(-x))


def _interleave(xs, ys):
    if not xs:
        return list(ys)
    out, j = [], 0
    for i, unit in enumerate(xs):
        out.append(unit)
        upto = (i + 1) * len(ys) // len(xs)
        out += ys[j:upto]
        j = upto
    return out + ys[j:]


def _resident(shape):
    nd = len(shape)
    return pl.BlockSpec(shape, lambda *_: (0,) * nd, pipeline_mode=pl.Buffered(1))


def _params(semantics):
    return pltpu.CompilerParams(dimension_semantics=semantics, vmem_limit_bytes=VMEM_LIMIT)


def _launch(body, *, steps, inputs, in_specs, out_spec, out_shape, scratch, semantics, name,
            casts=()):
    n_in, ncast = len(inputs), len(casts)

    def kern(*refs):
        cast_in = refs[n_in:n_in + ncast]
        cast_out = refs[n_in + ncast + 1:n_in + 2 * ncast + 1]
        for src, dst in zip(cast_in, cast_out):
            dst[...] = src[...].astype(BF16)
        body(*refs[:n_in], refs[n_in + ncast], *refs[n_in + 2 * ncast + 1:])

    cast_in_specs, cast_out_specs, cast_shapes = [], [], []
    for w, layer in casts:
        _, rows, cols = w.shape
        rb = rows // steps
        assert rb * steps == rows and rb % (2 * SUBLANES) == 0, (w.shape, steps)
        cast_in_specs.append(pl.BlockSpec((None, rb, cols), lambda i, layer=layer: (layer, i, 0)))
        cast_out_specs.append(pl.BlockSpec((rb, cols), lambda i: (i, 0)))
        cast_shapes.append(jax.ShapeDtypeStruct((rows, cols), BF16))
    res = pl.pallas_call(
        kern,
        grid=(steps,),
        in_specs=list(in_specs) + cast_in_specs,
        out_specs=[out_spec] + cast_out_specs,
        out_shape=[out_shape] + cast_shapes,
        scratch_shapes=scratch,
        compiler_params=_params((semantics,)),
        name=name,
    )(*inputs, *[w for w, _ in casts])
    return res[0], list(res[1:])


def _mlp_kernel(x_ref, g_ref, w1_ref, w2_ref, fg_ref, o_ref, hn_scr, a_scr, *,
                final_norm, nf, tr):
    dff = w1_ref.shape[1]
    for r0 in range(0, x_ref.shape[0], tr):
        rows = slice(r0, r0 + tr)
        hn_scr[rows, :] = _rms(x_ref[rows, :], g_ref[...]).astype(BF16)
        for n0 in range(0, dff, nf):
            a = jnp.maximum(_dot(hn_scr[rows, :], w1_ref[:, n0:n0 + nf]), 0.0)
            a_scr[rows, n0:n0 + nf] = (a * a).astype(BF16)
    for r0 in range(0, x_ref.shape[0], tr):
        rows = slice(r0, r0 + tr)
        y = x_ref[rows, :] + _dot(a_scr[rows, :], w2_ref[...])
        if final_norm:
            y = _rms(y, fg_ref[...])
        o_ref[rows, :] = y


def _mlp(x2, g, w1, w2, final_g, *, final_norm, casts=(), tm=1024, tr=512, nf=256):
    n, d = x2.shape
    dff = w1.shape[1]
    tm = min(tm, n)
    return _launch(
        functools.partial(_mlp_kernel, final_norm=final_norm, nf=nf, tr=min(tr, tm)),
        steps=n // tm,
        inputs=(x2, g, w1, w2, final_g),
        in_specs=[
            pl.BlockSpec((tm, d), lambda i: (i, 0)),
            _resident((1, d)),
            _resident(w1.shape),
            _resident(w2.shape),
            _resident((1, d)),
        ],
        out_spec=pl.BlockSpec((tm, d), lambda i: (i, 0)),
        out_shape=jax.ShapeDtypeStruct((n, d), F32),
        scratch=[pltpu.VMEM((tm, d), BF16), pltpu.VMEM((tm, dff), BF16)],
        semantics="parallel",
        name="mlp_final" if final_norm else "mlp",
        casts=casts,
    )


def _gmlp_kernel(x_ref, g_ref, win_ref, lng_ref, ws_ref, bs_ref, wout_ref, o_ref,
                 hn_scr, u_scr, v_scr, vb_scr, gated_scr, *, tr):
    tc = x_ref.shape[0]
    e = lng_ref.shape[1]
    n_groups = ws_ref.shape[0]
    gd = e // n_groups
    nb = 256
    row = lax.broadcasted_iota(jnp.int32, (CHUNK, CHUNK), 0)
    col = lax.broadcasted_iota(jnp.int32, (CHUNK, CHUNK), 1)
    ws = [jnp.where(row >= col, ws_ref[g], 0.0).astype(BF16) for g in range(n_groups)]

    def stages(r0):
        rows = slice(r0, r0 + tr)

        def norm():
            hn_scr[rows, :] = _rms(x_ref[rows, :], g_ref[...]).astype(BF16)

        def v_unit(c0):
            v_scr[rows, c0:c0 + nb] = _gelu(_dot(hn_scr[rows, :], win_ref[:, e + c0:e + c0 + nb]))

        def ln_unit(q0):
            v = v_scr[q0:q0 + CHUNK, :]
            mu = jnp.mean(v, axis=-1, keepdims=True)
            dv = v - mu
            var = jnp.mean(dv * dv, axis=-1, keepdims=True)
            vb_scr[q0:q0 + CHUNK, :] = (dv * lax.rsqrt(var + LN_EPS) * lng_ref[...]).astype(BF16)

        def u_unit(c0):
            u_scr[rows, c0:c0 + nb] = _gelu(_dot(hn_scr[rows, :], win_ref[:, c0:c0 + nb]))

        def mix_unit(g, q0):
            q, cols = slice(q0, q0 + CHUNK), slice(g * gd, (g + 1) * gd)
            sv = _dot(ws[g], vb_scr[q, cols]) + bs_ref[g]
            gated_scr[q, cols] = (u_scr[q, cols] * sv).astype(BF16)

        def out_unit(c0):
            o_ref[rows, c0:c0 + nb] = (x_ref[rows, c0:c0 + nb]
                                       + _dot(gated_scr[rows, :], wout_ref[:, c0:c0 + nb]))

        part = functools.partial
        chunks = range(r0, r0 + tr, CHUNK)
        return [
            [norm] + [part(v_unit, c0) for c0 in range(0, e, nb)],
            [part(ln_unit, q0) for q0 in chunks],
            [part(u_unit, c0) for c0 in range(0, e, nb)],
            [part(mix_unit, g, q0) for g in range(n_groups) for q0 in chunks],
            [part(out_unit, c0) for c0 in range(0, x_ref.shape[1], nb)],
        ]

    pipes = [stages(r0) for r0 in range(0, tc, tr)]
    nstage = len(pipes[0])
    for slot in range(nstage + len(pipes) - 1):
        units = []
        for p, pipe in enumerate(pipes):
            if 0 <= slot - p < nstage:
                units = _interleave(units, pipe[slot - p]) if units else list(pipe[slot - p])
        for unit in units:
            unit()


def _gmlp(x2, g, w_in, ln_g, ws, bs_b, w_out, *, casts=(), tc=512, tr=256):
    n, d = x2.shape
    e = ln_g.shape[1]
    tc = min(tc, n)
    return _launch(
        functools.partial(_gmlp_kernel, tr=min(tr, tc)),
        steps=n // tc,
        inputs=(x2, g, w_in, ln_g, ws, bs_b, w_out),
        in_specs=[
            pl.BlockSpec((tc, d), lambda i: (i, 0)),
            _resident((1, d)),
            _resident(w_in.shape),
            _resident((1, e)),
            _resident(ws.shape),
            _resident(bs_b.shape),
            _resident(w_out.shape),
        ],
        out_spec=pl.BlockSpec((tc, d), lambda i: (i, 0)),
        out_shape=jax.ShapeDtypeStruct((n, d), F32),
        scratch=[pltpu.VMEM((tc, d), BF16),
                 pltpu.VMEM((tc, e), F32), pltpu.VMEM((tc, e), F32),
                 pltpu.VMEM((tc, e), BF16), pltpu.VMEM((tc, e), BF16)],
        semantics="parallel",
        name="gmlp_mixer",
        casts=casts,
    )


def _s5_pitch(tc):
    assert tc % SUBLANES == 0
    return tc + SUBLANES // 2


def _s5_kernel(x_ref, g_ref, win_ref, wb_ref, a_ref, wc_ref, d_ref, wglu_ref, o_ref,
               hn_scr, u_scr, h_scr, hc_scr, z_scr, *s_scr, pitch):
    nbatch, tc, d = x_ref.shape
    e = d_ref.shape[1]
    ncb = wb_ref.shape[0]
    cw = S5_COLBLOCK_GROUPS * S5_GROUP
    ns = S5_COLBLOCK_GROUPS * S5_STATE
    nslab = ns // LANES

    @pl.when(pl.program_id(0) == 0)
    def _():
        h_scr[...] = jnp.zeros_like(h_scr)

    nw = 2 * LANES

    def skew(idx):
        return idx % (SUBLANES // 2)

    ar = [a_ref[0, cb] for cb in range(ncb)]
    ai = [a_ref[1, cb] for cb in range(ncb)]
    state = [[h_scr[b * 2 * ncb + i] for i in range(2 * ncb)] for b in range(nbatch)]


    def front_units(b):
        units = []

        def norm():
            hn_scr[b] = _rms(x_ref[b], g_ref[...]).astype(BF16)
        units.append(norm)
        for c0 in range(0, e, 2 * nw):
            def proj(c0=c0):
                u_scr[b, :, c0:c0 + 2 * nw] = _dot(hn_scr[b], win_ref[:, c0:c0 + 2 * nw])
            units.append(proj)
        for cb in range(ncb):
            for c0 in range(0, 2 * ns, nw):
                def bu_unit(cb=cb, c0=c0):
                    ub = u_scr[b, :, cb * cw:(cb + 1) * cw].astype(BF16)
                    bu = _dot(ub, wb_ref[cb, :, c0:c0 + nw])
                    part, k0 = c0 // ns, (c0 % ns) // LANES
                    for kk in range(nw // LANES):
                        r0 = (k0 + kk) * pitch + skew(cb * 2 + part)
                        s_scr[b][cb * 2 + part, r0:r0 + tc, :] = bu[:, kk * LANES:(kk + 1) * LANES]
                units.append(bu_unit)
        return units

    def scan_units(b, group=8):
        def make(t0):
            def run():
                for t in range(t0, t0 + group):
                    for cb in range(ncb):
                        slr = pl.ds(t + skew(2 * cb), SUBLANES, stride=pitch)
                        sli = pl.ds(t + skew(2 * cb + 1), SUBLANES, stride=pitch)
                        hr, hi = state[b][2 * cb], state[b][2 * cb + 1]
                        nr = ar[cb] * hr - ai[cb] * hi + s_scr[b][2 * cb, slr, :]
                        ni = ar[cb] * hi + ai[cb] * hr + s_scr[b][2 * cb + 1, sli, :]
                        s_scr[b][2 * cb, slr, :] = nr
                        s_scr[b][2 * cb + 1, sli, :] = ni
                        state[b][2 * cb], state[b][2 * cb + 1] = nr, ni
            return run
        return [make(t0) for t0 in range(0, tc, group)]

    def back_units(b):
        units = []
        for cb in range(ncb):
            def gather(cb=cb):
                for part in range(2):
                    for k in range(nslab):
                        c0 = part * ns + k * LANES
                        r0 = k * pitch + skew(cb * 2 + part)
                        hc_scr[b, cb % 2, :, c0:c0 + LANES] = (
                            s_scr[b][cb * 2 + part, r0:r0 + tc, :].astype(BF16))
            units.append(gather)

            def out_proj(cb=cb):
                cols = slice(cb * cw, (cb + 1) * cw)
                y = _dot(hc_scr[b, cb % 2], wc_ref[cb]) + d_ref[:, cols] * u_scr[b, :, cols]
                z_scr[b, :, cols] = _gelu(y).astype(BF16)
            units.append(out_proj)
        for c0 in range(0, d, nw):
            def glu(c0=c0):
                val = _dot(z_scr[b], wglu_ref[:, c0:c0 + nw])
                gate = _dot(z_scr[b], wglu_ref[:, d + c0:d + c0 + nw])
                o_ref[b, :, c0:c0 + nw] = x_ref[b, :, c0:c0 + nw] + val * _sigmoid(gate)
            units.append(glu)
        return units

    for slot in range(nbatch + 2):
        mxu = []
        if slot < nbatch:
            mxu += front_units(slot)
        if 0 <= slot - 2 < nbatch:
            mxu += back_units(slot - 2)
        vpu = scan_units(slot - 1) if 0 <= slot - 1 < nbatch else []
        for unit in _interleave(mxu, vpu):
            unit()
    for b in range(nbatch):
        for i in range(2 * ncb):
            h_scr[b * 2 * ncb + i] = state[b][i]


def _s5(x, g, w_in, wb, a_tiles, wc, d_skip, w_glu, *, casts=(), tc=256):
    nbatch, l, d = x.shape
    e = d_skip.shape[1]
    ncb = wb.shape[0]
    tc = min(tc, l)
    pitch = _s5_pitch(tc)
    nslab = S5_COLBLOCK_GROUPS * S5_STATE // LANES
    return _launch(
        functools.partial(_s5_kernel, pitch=pitch),
        steps=l // tc,
        inputs=(x, g, w_in, wb, a_tiles, wc, d_skip, w_glu),
        in_specs=[
            pl.BlockSpec((nbatch, tc, d), lambda i: (0, i, 0)),
            _resident((1, d)),
            _resident(w_in.shape),
            _resident(wb.shape),
            _resident(a_tiles.shape),
            _resident(wc.shape),
            _resident((1, e)),
            _resident(w_glu.shape),
        ],
        out_spec=pl.BlockSpec((nbatch, tc, d), lambda i: (0, i, 0)),
        out_shape=jax.ShapeDtypeStruct((nbatch, l, d), F32),
        scratch=[
            pltpu.VMEM((nbatch, tc, d), BF16),
            pltpu.VMEM((nbatch, tc, e), F32),
            pltpu.VMEM((nbatch * ncb * 2, SUBLANES, LANES), F32),
            pltpu.VMEM((nbatch, 2, tc, 2 * nslab * LANES), BF16),
            pltpu.VMEM((nbatch, tc, e), BF16),
        ] + [pltpu.VMEM((ncb * 2, nslab * pitch + SUBLANES, LANES), F32) for _ in range(nbatch)],
        semantics="arbitrary",
        name="s5_mixer",
        casts=casts,
    )


def _s5_prepare(log_dt, a_re, a_im, b_re, b_im, c_re, c_im):
    ngroups, nstate = a_re.shape
    m = b_re.shape[2]
    dt = jnp.exp(log_dt.astype(F32))[:, None]
    ar, ai = a_re.astype(F32), a_im.astype(F32)
    mag = jnp.exp(ar * dt)
    ang = ai * dt
    abar_r, abar_i = mag * jnp.cos(ang), mag * jnp.sin(ang)
    den = ar * ar + ai * ai
    nr, ni = abar_r - 1.0, abar_i
    cr = (nr * ar + ni * ai) / den
    ci = (ni * ar - nr * ai) / den
    br, bi = b_re.astype(F32), b_im.astype(F32)
    bbar_r = cr[..., None] * br - ci[..., None] * bi
    bbar_i = cr[..., None] * bi + ci[..., None] * br
    gpb = S5_COLBLOCK_GROUPS
    ncb = ngroups // gpb

    def blockdiag(w):
        _, na, nb = w.shape
        wt = w.reshape(ncb, gpb, na, nb).transpose(0, 1, 3, 2).reshape(ncb, gpb * nb, na)
        r = lax.broadcasted_iota(jnp.int32, (gpb * nb, gpb * na), 0) // nb
        c = lax.broadcasted_iota(jnp.int32, (gpb * nb, gpb * na), 1) // na
        return jnp.where(r == c, jnp.tile(wt, (1, 1, gpb)), 0.0)

    wb = jnp.concatenate([blockdiag(bbar_r), blockdiag(bbar_i)], axis=2).astype(BF16)
    wc = jnp.concatenate([blockdiag(c_re.astype(F32)),
                          blockdiag(-c_im.astype(F32))], axis=1).astype(BF16)
    a_tiles = jnp.stack([abar_r, abar_i]).reshape(2, ncb, SUBLANES, LANES)
    return wb, a_tiles, wc


def _ret_kernel(x_ref, g_ref, win_ref, cc_ref, sc_ref, ct_ref, st_ref, gng_ref, wout_ref, o_ref,
                hn_scr, r_scr, gated_scr, dmask_scr, xi_scr, zeta_scr):
    nbatch, tc, d = x_ref.shape
    nheads = RET_HEADS
    dk = r_scr.shape[1]
    dv = r_scr.shape[2]
    qd = nheads * dk
    vd = nheads * dv
    half = dk // 2
    log_g = [math.log(1.0 - 2.0 ** (-5.0 - h)) for h in range(nheads)]

    @pl.when(pl.program_id(0) == 0)
    def _():
        r_scr[...] = jnp.zeros_like(r_scr)
        row = lax.broadcasted_iota(jnp.int32, (tc, tc), 0)
        col = lax.broadcasted_iota(jnp.int32, (tc, tc), 1)
        diff = (row - col).astype(F32)
        t = lax.broadcasted_iota(jnp.int32, (tc, LANES), 0).astype(F32)
        for h in range(nheads):
            dmask_scr[h] = jnp.where(diff >= 0, jnp.exp(jnp.maximum(diff, 0.0) * log_g[h]), 0.0)
            xi_scr[h] = jnp.exp((t + 1.0) * log_g[h])
            zeta_scr[h] = jnp.exp((tc - 1.0 - t) * log_g[h])

    for b in range(nbatch):
        hn_scr[b * tc:(b + 1) * tc, :] = _rms(x_ref[b], g_ref[...]).astype(BF16)

    cc, sc, ct, st = cc_ref[...], sc_ref[...], ct_ref[...], st_ref[...]
    cos = jnp.concatenate([cc * ct - sc * st] * nbatch, axis=0)
    sin = jnp.concatenate([sc * ct + cc * st] * nbatch, axis=0)

    def rotate(t):
        t1, t2 = t[:, :half], t[:, half:]
        return jnp.concatenate([t1 * cos - t2 * sin, t1 * sin + t2 * cos], axis=1)

    for h in range(nheads):
        dmask = dmask_scr[h]
        xi = jnp.concatenate([xi_scr[h]] * (dv // LANES), axis=1)
        zeta = jnp.concatenate([zeta_scr[h]] * (dk // LANES), axis=1)
        hn = hn_scr[...]
        q = rotate(_dot(hn, win_ref[:, h * dk:(h + 1) * dk])).astype(BF16)
        k = rotate(_dot(hn, win_ref[:, qd + h * dk:qd + (h + 1) * dk])) * (dk ** -0.5)
        v = _dot(hn, win_ref[:, 2 * qd + h * dv:2 * qd + (h + 1) * dv]).astype(BF16)
        gate = _dot(hn, win_ref[:, 2 * qd + vd + h * dv:2 * qd + vd + (h + 1) * dv])
        for b in range(nbatch):
            rows = slice(b * tc, (b + 1) * tc)
            kb = k[rows]
            s = lax.dot_general(q[rows], kb.astype(BF16), (((1,), (1,)), ((), ())),
                                preferred_element_type=F32) * dmask
            r_old = r_scr[b * nheads + h]
            o = _dot(s.astype(BF16), v[rows]) + _dot(q[rows], r_old.astype(BF16)) * xi
            r_scr[b * nheads + h] = math.exp(tc * log_g[h]) * r_old + lax.dot_general(
                (kb * zeta).astype(BF16), v[rows], (((0,), (0,)), ((), ())),
                preferred_element_type=F32)
            mu = jnp.mean(o, axis=-1, keepdims=True)
            do = o - mu
            var = jnp.mean(do * do, axis=-1, keepdims=True)
            on = do * lax.rsqrt(var + LN_EPS) * gng_ref[:, h * dv:(h + 1) * dv]
            gb = gate[rows]
            gated_scr[rows, h * dv:(h + 1) * dv] = (gb * _sigmoid(gb) * on).astype(BF16)
    nw = 512
    for c0 in range(0, d, nw):
        res = _dot(gated_scr[...], wout_ref[:, c0:c0 + nw])
        for b in range(nbatch):
            o_ref[b, :, c0:c0 + nw] = x_ref[b, :, c0:c0 + nw] + res[b * tc:(b + 1) * tc]


def _retention(x, g, w_in, gn_g, w_out, *, casts=(), tc=256):
    nbatch, l, d = x.shape
    tc = min(tc, l)
    dk = d // RET_HEADS
    dv = 2 * dk
    vd = RET_HEADS * dv
    inv_freq = ROPE_BASE ** (-jnp.arange(0, dk, 2, dtype=F32) / dk)
    th_c = (jnp.arange(l // tc, dtype=F32) * tc)[:, None, None] * inv_freq
    th_t = jnp.arange(tc, dtype=F32)[:, None] * inv_freq
    chunk_spec = pl.BlockSpec((None, 1, dk // 2), lambda i: (i, 0, 0))
    return _launch(
        _ret_kernel,
        steps=l // tc,
        inputs=(x, g, w_in, jnp.cos(th_c), jnp.sin(th_c), jnp.cos(th_t), jnp.sin(th_t), gn_g, w_out),
        in_specs=[
            pl.BlockSpec((nbatch, tc, d), lambda i: (0, i, 0)),
            _resident((1, d)),
            _resident(w_in.shape),
            chunk_spec,
            chunk_spec,
            _resident((tc, dk // 2)),
            _resident((tc, dk // 2)),
            _resident((1, vd)),
            _resident(w_out.shape),
        ],
        out_spec=pl.BlockSpec((nbatch, tc, d), lambda i: (0, i, 0)),
        out_shape=jax.ShapeDtypeStruct((nbatch, l, d), F32),
        scratch=[pltpu.VMEM((nbatch * tc, d), BF16),
                 pltpu.VMEM((nbatch * RET_HEADS, dk, dv), F32),
                 pltpu.VMEM((nbatch * tc, vd), BF16),
                 pltpu.VMEM((RET_HEADS, tc, tc), F32),
                 pltpu.VMEM((RET_HEADS, tc, LANES), F32),
                 pltpu.VMEM((RET_HEADS, tc, LANES), F32)],
        semantics="arbitrary",
        name="retention_mixer",
        casts=casts,
    )


def _sconv_kernel(x_ref, g_ref, win_ref, cw_ref, wout_ref, o_ref, hn_scr, zs_scr, gy_scr):
    nbatch, tc, d = x_ref.shape
    e = cw_ref.shape[1]
    pad = SUBLANES

    @pl.when(pl.program_id(0) == 0)
    def _():
        zs_scr[:, 0:pad, :] = jnp.zeros((nbatch, pad, e), F32)

    for b in range(nbatch):
        hn_scr[b * tc:(b + 1) * tc, :] = _rms(x_ref[b], g_ref[...]).astype(BF16)
    nw = 512
    for c0 in range(0, e, nw):
        cols = slice(c0, c0 + nw)
        hn = hn_scr[...]
        gate_b = _dot(hn, win_ref[:, c0:c0 + nw])
        z = (_dot(hn, win_ref[:, e + c0:e + c0 + nw])
             * _dot(hn, win_ref[:, 2 * e + c0:2 * e + c0 + nw]))
        for b in range(nbatch):
            rows = slice(b * tc, (b + 1) * tc)
            zs_scr[b, pad:pad + tc, cols] = z[rows]
            y = (cw_ref[0:1, cols] * zs_scr[b, pad - 2:pad - 2 + tc, cols]
                 + cw_ref[1:2, cols] * zs_scr[b, pad - 1:pad - 1 + tc, cols]
                 + cw_ref[2:3, cols] * z[rows])
            zs_scr[b, 0:pad, cols] = zs_scr[b, tc:tc + pad, cols]
            gy_scr[rows, cols] = (gate_b[rows] * y).astype(BF16)
    for c0 in range(0, d, nw):
        res = _dot(gy_scr[...], wout_ref[:, c0:c0 + nw])
        for b in range(nbatch):
            o_ref[b, :, c0:c0 + nw] = x_ref[b, :, c0:c0 + nw] + res[b * tc:(b + 1) * tc]


def _sconv(x, g, w_in, conv_w, w_out, *, casts=(), tc=512):
    nbatch, l, d = x.shape
    e = conv_w.shape[1]
    tc = min(tc, l)
    return _launch(
        _sconv_kernel,
        steps=l // tc,
        inputs=(x, g, w_in, conv_w, w_out),
        in_specs=[
            pl.BlockSpec((nbatch, tc, d), lambda i: (0, i, 0)),
            _resident((1, d)),
            _resident(w_in.shape),
            _resident(conv_w.shape),
            _resident(w_out.shape),
        ],
        out_spec=pl.BlockSpec((nbatch, tc, d), lambda i: (0, i, 0)),
        out_shape=jax.ShapeDtypeStruct((nbatch, l, d), F32),
        scratch=[pltpu.VMEM((nbatch * tc, d), BF16),
                 pltpu.VMEM((nbatch, tc + SUBLANES, e), F32),
                 pltpu.VMEM((nbatch * tc, e), BF16)],
        semantics="arbitrary",
        name="sconv_mixer",
        casts=casts,
    )


def kernel(x, norm1_g, norm2_g, mlp_w1, mlp_w2, final_g, a_w_in, a_ln_g, a_ws, a_bs, a_w_out, b_w_in, b_log_dt, b_a_re, b_a_im, b_b_re, b_b_im, b_c_re, b_c_im, b_d, b_w_glu, c_w_in, c_gn_g, c_w_out, d_w_in, d_conv_w, d_w_out):
    nbatch, l, d = x.shape
    depth = norm1_g.shape[0]
    n_mixers = 4
    row = lambda v: v.reshape(1, -1).astype(F32)

    def mixer_weights(i):
        m, j = i % n_mixers, i // n_mixers
        pairs = ((a_w_in, a_w_out), (b_w_in, b_w_glu), (c_w_in, c_w_out), (d_w_in, d_w_out))[m]
        return [(w, j) for w in pairs]

    w_in, w_out = (w[j].astype(BF16) for w, j in mixer_weights(0))
    for i in range(depth):
        m, j = i % n_mixers, i // n_mixers
        g1 = row(norm1_g[i])
        mlp_casts = [(mlp_w1, i), (mlp_w2, i)]
        if m == 0:
            gd = a_ln_g.shape[1] // GM_GROUPS
            bs_b = jnp.broadcast_to(a_bs[j].astype(F32)[:, :, None], (GM_GROUPS, CHUNK, gd))
            x, (w1, w2) = _gmlp(x.reshape(nbatch * l, d), g1, w_in, row(a_ln_g[j]),
                                a_ws[j].astype(F32), bs_b, w_out, casts=mlp_casts)
        elif m == 1:
            wb, a_tiles, wc = _s5_prepare(b_log_dt[j], b_a_re[j], b_a_im[j], b_b_re[j],
                                          b_b_im[j], b_c_re[j], b_c_im[j])
            x, (w1, w2) = _s5(x, g1, w_in, wb, a_tiles, wc, row(b_d[j]), w_out, casts=mlp_casts)
        elif m == 2:
            x, (w1, w2) = _retention(x, g1, w_in, row(c_gn_g[j]), w_out, casts=mlp_casts)
        else:
            x, (w1, w2) = _sconv(x, g1, w_in, d_conv_w[j].reshape(CONV_WIDTH, -1).astype(F32),
                                 w_out, casts=mlp_casts)
        last = i == depth - 1
        x, nxt = _mlp(x.reshape(nbatch * l, d), row(norm2_g[i]), w1, w2, row(final_g),
                      final_norm=last, casts=() if last else mixer_weights(i + 1))
        x = x.reshape(nbatch, l, d)
        if not last:
            w_in, w_out = nxt
    return x
```

```python
import functools
import math

import jax
import jax.numpy as jnp
from jax import lax
from jax.experimental import pallas as pl
from jax.experimental.pallas import tpu as pltpu

F32 = jnp.float32
BF16 = jnp.bfloat16

EPS = 1e-6
LN_EPS = 1e-5
CHUNK = 128
GM_GROUPS = 8
S5_GROUP = 16
S5_STATE = 64
S5_COLBLOCK_GROUPS = 16
RET_HEADS = 4
ROPE_BASE = 10000.0
CONV_WIDTH = 3

LANES = 128
SUBLANES = 8
VMEM_LIMIT = 56 * 1024 * 1024


def _dot(a, b):
    return jnp.dot(a, b, preferred_element_type=F32)


def _rms(x, g):
    return x * lax.rsqrt(jnp.mean(x * x, axis=-1, keepdims=True) + EPS) * g


def _gelu(x):
    c = math.sqrt(2.0 / math.pi)
    return x * (0.5 + 0.5 * jnp.tanh(x * (c + (c * 0.044715) * (x * x))))


def _sigmoid(x):
    return 1.0 / (1.0 + jnp.exp(-x))


def _interleave(xs, ys):
    if not xs:
        return list(ys)
    out, j = [], 0
    for i, unit in enumerate(xs):
        out.append(unit)
        upto = (i + 1) * len(ys) // len(xs)
        out += ys[j:upto]
        j = upto
    return out + ys[j:]


def _resident(shape):
    nd = len(shape)
    return pl.BlockSpec(shape, lambda *_: (0,) * nd, pipeline_mode=pl.Buffered(1))


def _params(semantics):
    return pltpu.CompilerParams(dimension_semantics=semantics, vmem_limit_bytes=VMEM_LIMIT)


def _launch(body, *, steps, inputs, in_specs, out_spec, out_shape, scratch, semantics, name,
            casts=()):
    n_in, ncast = len(inputs), len(casts)

    def kern(*refs):
        cast_in = refs[n_in:n_in + ncast]
        cast_out = refs[n_in + ncast + 1:n_in + 2 * ncast + 1]
        for src, dst in zip(cast_in, cast_out):
            dst[...] = src[...].astype(BF16)
        body(*refs[:n_in], refs[n_in + ncast], *refs[n_in + 2 * ncast + 1:])

    cast_in_specs, cast_out_specs, cast_shapes = [], [], []
    for w, layer in casts:
        _, rows, cols = w.shape
        rb = rows // steps
        assert rb * steps == rows and rb % (2 * SUBLANES) == 0, (w.shape, steps)
        cast_in_specs.append(pl.BlockSpec((None, rb, cols), lambda i, layer=layer: (layer, i, 0)))
        cast_out_specs.append(pl.BlockSpec((rb, cols), lambda i: (i, 0)))
        cast_shapes.append(jax.ShapeDtypeStruct((rows, cols), BF16))
    res = pl.pallas_call(
        kern,
        grid=(steps,),
        in_specs=list(in_specs) + cast_in_specs,
        out_specs=[out_spec] + cast_out_specs,
        out_shape=[out_shape] + cast_shapes,
        scratch_shapes=scratch,
        compiler_params=_params((semantics,)),
        name=name,
    )(*inputs, *[w for w, _ in casts])
    return res[0], list(res[1:])


def _mlp_kernel(x_ref, g_ref, w1_ref, w2_ref, fg_ref, o_ref, hn_scr, a_scr, *,
                final_norm, nf, tr):
    dff = w1_ref.shape[1]
    for r0 in range(0, x_ref.shape[0], tr):
        rows = slice(r0, r0 + tr)
        hn_scr[rows, :] = _rms(x_ref[rows, :], g_ref[...]).astype(BF16)
        for n0 in range(0, dff, nf):
            a = jnp.maximum(_dot(hn_scr[rows, :], w1_ref[:, n0:n0 + nf]), 0.0)
            a_scr[rows, n0:n0 + nf] = (a * a).astype(BF16)
    for r0 in range(0, x_ref.shape[0], tr):
        rows = slice(r0, r0 + tr)
        y = x_ref[rows, :] + _dot(a_scr[rows, :], w2_ref[...])
        if final_norm:
            y = _rms(y, fg_ref[...])
        o_ref[rows, :] = y


def _mlp(x2, g, w1, w2, final_g, *, final_norm, casts=(), tm=1024, tr=512, nf=256):
    n, d = x2.shape
    dff = w1.shape[1]
    tm = min(tm, n)
    return _launch(
        functools.partial(_mlp_kernel, final_norm=final_norm, nf=nf, tr=min(tr, tm)),
        steps=n // tm,
        inputs=(x2, g, w1, w2, final_g),
        in_specs=[
            pl.BlockSpec((tm, d), lambda i: (i, 0)),
            _resident((1, d)),
            _resident(w1.shape),
            _resident(w2.shape),
            _resident((1, d)),
        ],
        out_spec=pl.BlockSpec((tm, d), lambda i: (i, 0)),
        out_shape=jax.ShapeDtypeStruct((n, d), F32),
        scratch=[pltpu.VMEM((tm, d), BF16), pltpu.VMEM((tm, dff), BF16)],
        semantics="parallel",
        name="mlp_final" if final_norm else "mlp",
        casts=casts,
    )


def _gmlp_kernel(x_ref, g_ref, win_ref, lng_ref, ws_ref, bs_ref, wout_ref, o_ref,
                 hn_scr, u_scr, v_scr, vb_scr, gated_scr, *, tr):
    tc = x_ref.shape[0]
    e = lng_ref.shape[1]
    n_groups = ws_ref.shape[0]
    gd = e // n_groups
    nb = 256
    row = lax.broadcasted_iota(jnp.int32, (CHUNK, CHUNK), 0)
    col = lax.broadcasted_iota(jnp.int32, (CHUNK, CHUNK), 1)
    ws = [jnp.where(row >= col, ws_ref[g], 0.0).astype(BF16) for g in range(n_groups)]

    def stages(r0):
        rows = slice(r0, r0 + tr)

        def norm():
            hn_scr[rows, :] = _rms(x_ref[rows, :], g_ref[...]).astype(BF16)

        def v_unit(c0):
            v_scr[rows, c0:c0 + nb] = _gelu(_dot(hn_scr[rows, :], win_ref[:, e + c0:e + c0 + nb]))

        def ln_unit(q0):
            v = v_scr[q0:q0 + CHUNK, :]
            mu = jnp.mean(v, axis=-1, keepdims=True)
            dv = v - mu
            var = jnp.mean(dv * dv, axis=-1, keepdims=True)
            vb_scr[q0:q0 + CHUNK, :] = (dv * lax.rsqrt(var + LN_EPS) * lng_ref[...]).astype(BF16)

        def u_unit(c0):
            u_scr[rows, c0:c0 + nb] = _gelu(_dot(hn_scr[rows, :], win_ref[:, c0:c0 + nb]))

        def mix_unit(g, q0):
            q, cols = slice(q0, q0 + CHUNK), slice(g * gd, (g + 1) * gd)
            sv = _dot(ws[g], vb_scr[q, cols]) + bs_ref[g]
            gated_scr[q, cols] = (u_scr[q, cols] * sv).astype(BF16)

        def out_unit(c0):
            o_ref[rows, c0:c0 + nb] = (x_ref[rows, c0:c0 + nb]
                                       + _dot(gated_scr[rows, :], wout_ref[:, c0:c0 + nb]))

        part = functools.partial
        chunks = range(r0, r0 + tr, CHUNK)
        return [
            [norm] + [part(v_unit, c0) for c0 in range(0, e, nb)],
            [part(ln_unit, q0) for q0 in chunks],
            [part(u_unit, c0) for c0 in range(0, e, nb)],
            [part(mix_unit, g, q0) for g in range(n_groups) for q0 in chunks],
            [part(out_unit, c0) for c0 in range(0, x_ref.shape[1], nb)],
        ]

    pipes = [stages(r0) for r0 in range(0, tc, tr)]
    nstage = len(pipes[0])
    for slot in range(nstage + len(pipes) - 1):
        units = []
        for p, pipe in enumerate(pipes):
            if 0 <= slot - p < nstage:
                units = _interleave(units, pipe[slot - p]) if units else list(pipe[slot - p])
        for unit in units:
            unit()


def _gmlp(x2, g, w_in, ln_g, ws, bs_b, w_out, *, casts=(), tc=512, tr=256):
    n, d = x2.shape
    e = ln_g.shape[1]
    tc = min(tc, n)
    return _launch(
        functools.partial(_gmlp_kernel, tr=min(tr, tc)),
        steps=n // tc,
        inputs=(x2, g, w_in, ln_g, ws, bs_b, w_out),
        in_specs=[
            pl.BlockSpec((tc, d), lambda i: (i, 0)),
            _resident((1, d)),
            _resident(w_in.shape),
            _resident((1, e)),
            _resident(ws.shape),
            _resident(bs_b.shape),
            _resident(w_out.shape),
        ],
        out_spec=pl.BlockSpec((tc, d), lambda i: (i, 0)),
        out_shape=jax.ShapeDtypeStruct((n, d), F32),
        scratch=[pltpu.VMEM((tc, d), BF16),
                 pltpu.VMEM((tc, e), F32), pltpu.VMEM((tc, e), F32),
                 pltpu.VMEM((tc, e), BF16), pltpu.VMEM((tc, e), BF16)],
        semantics="parallel",
        name="gmlp_mixer",
        casts=casts,
    )


def _s5_pitch(tc):
    assert tc % SUBLANES == 0
    return tc + SUBLANES // 2


def _s5_kernel(x_ref, g_ref, win_ref, wb_ref, a_ref, wc_ref, d_ref, wglu_ref, o_ref,
               hn_scr, u_scr, ub_scr, h_scr, hc_scr, z_scr, *s_scr, pitch):
    nbatch, tc, d = x_ref.shape
    e = d_ref.shape[1]
    ncb = wb_ref.shape[0]
    cw = S5_COLBLOCK_GROUPS * S5_GROUP
    ns = S5_COLBLOCK_GROUPS * S5_STATE
    nslab = ns // LANES

    @pl.when(pl.program_id(0) == 0)
    def _():
        h_scr[...] = jnp.zeros_like(h_scr)

    nw = 2 * LANES

    def skew(idx):
        return idx % (SUBLANES // 2)

    ar = [a_ref[0, cb] for cb in range(ncb)]
    ai = [a_ref[1, cb] for cb in range(ncb)]
    state = [[h_scr[b * 2 * ncb + i] for i in range(2 * ncb)] for b in range(nbatch)]


    def front_units(b):
        units = []

        def norm():
            hn_scr[b] = _rms(x_ref[b], g_ref[...]).astype(BF16)
        units.append(norm)
        for c0 in range(0, e, 2 * nw):
            def proj(c0=c0):
                u = _dot(hn_scr[b], win_ref[:, c0:c0 + 2 * nw])
                u_scr[b, :, c0:c0 + 2 * nw] = u
                ub_scr[b, :, c0:c0 + 2 * nw] = u.astype(BF16)
            units.append(proj)
        for cb in range(ncb):
            for c0 in range(0, 2 * ns, nw):
                def bu_unit(cb=cb, c0=c0):
                    bu = _dot(ub_scr[b, :, cb * cw:(cb + 1) * cw], wb_ref[cb, :, c0:c0 + nw])
                    part, k0 = c0 // ns, (c0 % ns) // LANES
                    for kk in range(nw // LANES):
                        r0 = (k0 + kk) * pitch + skew(cb * 2 + part)
                        s_scr[b][cb * 2 + part, r0:r0 + tc, :] = bu[:, kk * LANES:(kk + 1) * LANES]
                units.append(bu_unit)
        return units

    def scan_units(b, group=8):
        def make(t0):
            def run():
                for t in range(t0, t0 + group):
                    for cb in range(ncb):
                        slr = pl.ds(t + skew(2 * cb), SUBLANES, stride=pitch)
                        sli = pl.ds(t + skew(2 * cb + 1), SUBLANES, stride=pitch)
                        hr, hi = state[b][2 * cb], state[b][2 * cb + 1]
                        nr = ar[cb] * hr - ai[cb] * hi + s_scr[b][2 * cb, slr, :]
                        ni = ar[cb] * hi + ai[cb] * hr + s_scr[b][2 * cb + 1, sli, :]
                        s_scr[b][2 * cb, slr, :] = nr
                        s_scr[b][2 * cb + 1, sli, :] = ni
                        state[b][2 * cb], state[b][2 * cb + 1] = nr, ni
            return run
        return [make(t0) for t0 in range(0, tc, group)]

    def back_units(b):
        units = []
        for cb in range(ncb):
            def gather(cb=cb):
                for part in range(2):
                    for k in range(nslab):
                        c0 = part * ns + k * LANES
                        r0 = k * pitch + skew(cb * 2 + part)
                        hc_scr[b, cb % 2, :, c0:c0 + LANES] = (
                            s_scr[b][cb * 2 + part, r0:r0 + tc, :].astype(BF16))
            units.append(gather)

            def out_proj(cb=cb):
                cols = slice(cb * cw, (cb + 1) * cw)
                y = _dot(hc_scr[b, cb % 2], wc_ref[cb]) + d_ref[:, cols] * u_scr[b, :, cols]
                z_scr[b, :, cols] = _gelu(y).astype(BF16)
            units.append(out_proj)
        for c0 in range(0, d, nw):
            def glu(c0=c0):
                val = _dot(z_scr[b], wglu_ref[:, c0:c0 + nw])
                gate = _dot(z_scr[b], wglu_ref[:, d + c0:d + c0 + nw])
                o_ref[b, :, c0:c0 + nw] = x_ref[b, :, c0:c0 + nw] + val * _sigmoid(gate)
            units.append(glu)
        return units

    for slot in range(nbatch + 2):
        mxu = []
        if slot < nbatch:
            mxu += front_units(slot)
        if 0 <= slot - 2 < nbatch:
            mxu += back_units(slot - 2)
        vpu = scan_units(slot - 1) if 0 <= slot - 1 < nbatch else []
        for unit in _interleave(mxu, vpu):
            unit()
    for b in range(nbatch):
        for i in range(2 * ncb):
            h_scr[b * 2 * ncb + i] = state[b][i]


def _s5(x, g, w_in, wb, a_tiles, wc, d_skip, w_glu, *, casts=(), tc=256):
    nbatch, l, d = x.shape
    e = d_skip.shape[1]
    ncb = wb.shape[0]
    tc = min(tc, l)
    pitch = _s5_pitch(tc)
    nslab = S5_COLBLOCK_GROUPS * S5_STATE // LANES
    return _launch(
        functools.partial(_s5_kernel, pitch=pitch),
        steps=l // tc,
        inputs=(x, g, w_in, wb, a_tiles, wc, d_skip, w_glu),
        in_specs=[
            pl.BlockSpec((nbatch, tc, d), lambda i: (0, i, 0)),
            _resident((1, d)),
            _resident(w_in.shape),
            _resident(wb.shape),
            _resident(a_tiles.shape),
            _resident(wc.shape),
            _resident((1, e)),
            _resident(w_glu.shape),
        ],
        out_spec=pl.BlockSpec((nbatch, tc, d), lambda i: (0, i, 0)),
        out_shape=jax.ShapeDtypeStruct((nbatch, l, d), F32),
        scratch=[
            pltpu.VMEM((nbatch, tc, d), BF16),
            pltpu.VMEM((nbatch, tc, e), F32),
            pltpu.VMEM((nbatch, tc, e), BF16),
            pltpu.VMEM((nbatch * ncb * 2, SUBLANES, LANES), F32),
            pltpu.VMEM((nbatch, 2, tc, 2 * nslab * LANES), BF16),
            pltpu.VMEM((nbatch, tc, e), BF16),
        ] + [pltpu.VMEM((ncb * 2, nslab * pitch + SUBLANES, LANES), F32) for _ in range(nbatch)],
        semantics="arbitrary",
        name="s5_mixer",
        casts=casts,
    )


def _s5_prepare(log_dt, a_re, a_im, b_re, b_im, c_re, c_im):
    ngroups, nstate = a_re.shape
    m = b_re.shape[2]
    dt = jnp.exp(log_dt.astype(F32))[:, None]
    ar, ai = a_re.astype(F32), a_im.astype(F32)
    mag = jnp.exp(ar * dt)
    ang = ai * dt
    abar_r, abar_i = mag * jnp.cos(ang), mag * jnp.sin(ang)
    den = ar * ar + ai * ai
    nr, ni = abar_r - 1.0, abar_i
    cr = (nr * ar + ni * ai) / den
    ci = (ni * ar - nr * ai) / den
    br, bi = b_re.astype(F32), b_im.astype(F32)
    bbar_r = cr[..., None] * br - ci[..., None] * bi
    bbar_i = cr[..., None] * bi + ci[..., None] * br
    gpb = S5_COLBLOCK_GROUPS
    ncb = ngroups // gpb

    def blockdiag(w):
        _, na, nb = w.shape
        wt = w.reshape(ncb, gpb, na, nb).transpose(0, 1, 3, 2).reshape(ncb, gpb * nb, na)
        r = lax.broadcasted_iota(jnp.int32, (gpb * nb, gpb * na), 0) // nb
        c = lax.broadcasted_iota(jnp.int32, (gpb * nb, gpb * na), 1) // na
        return jnp.where(r == c, jnp.tile(wt, (1, 1, gpb)), 0.0)

    wb = jnp.concatenate([blockdiag(bbar_r), blockdiag(bbar_i)], axis=2).astype(BF16)
    wc = jnp.concatenate([blockdiag(c_re.astype(F32)),
                          blockdiag(-c_im.astype(F32))], axis=1).astype(BF16)
    a_tiles = jnp.stack([abar_r, abar_i]).reshape(2, ncb, SUBLANES, LANES)
    return wb, a_tiles, wc


def _ret_kernel(x_ref, g_ref, win_ref, cc_ref, sc_ref, ct_ref, st_ref, gng_ref, wout_ref, o_ref,
                hn_scr, r_scr, gated_scr, dmask_scr, xi_scr, zeta_scr, v_scr, gate_scr):
    nbatch, tc, d = x_ref.shape
    nheads = RET_HEADS
    dk = r_scr.shape[1]
    dv = r_scr.shape[2]
    qd = nheads * dk
    vd = nheads * dv
    half = dk // 2
    log_g = [math.log(1.0 - 2.0 ** (-5.0 - h)) for h in range(nheads)]

    @pl.when(pl.program_id(0) == 0)
    def _():
        r_scr[...] = jnp.zeros_like(r_scr)
        row = lax.broadcasted_iota(jnp.int32, (tc, tc), 0)
        col = lax.broadcasted_iota(jnp.int32, (tc, tc), 1)
        diff = (row - col).astype(F32)
        t = lax.broadcasted_iota(jnp.int32, (tc, LANES), 0).astype(F32)
        for h in range(nheads):
            dmask_scr[h] = jnp.where(diff >= 0, jnp.exp(jnp.maximum(diff, 0.0) * log_g[h]), 0.0)
            xi_scr[h] = jnp.exp((t + 1.0) * log_g[h])
            zeta_scr[h] = jnp.exp((tc - 1.0 - t) * log_g[h])

    for b in range(nbatch):
        hn_scr[b * tc:(b + 1) * tc, :] = _rms(x_ref[b], g_ref[...]).astype(BF16)

    cc, sc, ct, st = cc_ref[...], sc_ref[...], ct_ref[...], st_ref[...]
    cos = jnp.concatenate([cc * ct - sc * st] * nbatch, axis=0)
    sin = jnp.concatenate([sc * ct + cc * st] * nbatch, axis=0)

    def rotate(t):
        t1, t2 = t[:, :half], t[:, half:]
        return jnp.concatenate([t1 * cos - t2 * sin, t1 * sin + t2 * cos], axis=1)

    for h in range(nheads):
        dmask = dmask_scr[h]
        xi = jnp.concatenate([xi_scr[h]] * (dv // LANES), axis=1)
        zeta = jnp.concatenate([zeta_scr[h]] * (dk // LANES), axis=1)
        hn = hn_scr[...]
        q = rotate(_dot(hn, win_ref[:, h * dk:(h + 1) * dk])).astype(BF16)
        k = rotate(_dot(hn, win_ref[:, qd + h * dk:qd + (h + 1) * dk])) * (dk ** -0.5)
        for c0 in range(0, dv, dk):
            cv = 2 * qd + h * dv + c0
            v_scr[:, c0:c0 + dk] = _dot(hn, win_ref[:, cv:cv + dk]).astype(BF16)
            gate_scr[:, c0:c0 + dk] = _dot(hn, win_ref[:, vd + cv:vd + cv + dk])
        v, gate = v_scr, gate_scr
        for b in range(nbatch):
            rows = slice(b * tc, (b + 1) * tc)
            kb = k[rows]
            s = lax.dot_general(q[rows], kb.astype(BF16), (((1,), (1,)), ((), ())),
                                preferred_element_type=F32) * dmask
            r_old = r_scr[b * nheads + h]
            o = _dot(s.astype(BF16), v[rows]) + _dot(q[rows], r_old.astype(BF16)) * xi
            r_scr[b * nheads + h] = math.exp(tc * log_g[h]) * r_old + lax.dot_general(
                (kb * zeta).astype(BF16), v[rows], (((0,), (0,)), ((), ())),
                preferred_element_type=F32)
            mu = jnp.mean(o, axis=-1, keepdims=True)
            do = o - mu
            var = jnp.mean(do * do, axis=-1, keepdims=True)
            on = do * lax.rsqrt(var + LN_EPS) * gng_ref[:, h * dv:(h + 1) * dv]
            gb = gate[rows]
            gated_scr[rows, h * dv:(h + 1) * dv] = (gb * _sigmoid(gb) * on).astype(BF16)
    nw = 512
    for c0 in range(0, d, nw):
        res = _dot(gated_scr[...], wout_ref[:, c0:c0 + nw])
        for b in range(nbatch):
            o_ref[b, :, c0:c0 + nw] = x_ref[b, :, c0:c0 + nw] + res[b * tc:(b + 1) * tc]


def _retention(x, g, w_in, gn_g, w_out, *, casts=(), tc=256):
    nbatch, l, d = x.shape
    tc = min(tc, l)
    dk = d // RET_HEADS
    dv = 2 * dk
    vd = RET_HEADS * dv
    inv_freq = ROPE_BASE ** (-jnp.arange(0, dk, 2, dtype=F32) / dk)
    th_c = (jnp.arange(l // tc, dtype=F32) * tc)[:, None, None] * inv_freq
    th_t = jnp.arange(tc, dtype=F32)[:, None] * inv_freq
    chunk_spec = pl.BlockSpec((None, 1, dk // 2), lambda i: (i, 0, 0))
    return _launch(
        _ret_kernel,
        steps=l // tc,
        inputs=(x, g, w_in, jnp.cos(th_c), jnp.sin(th_c), jnp.cos(th_t), jnp.sin(th_t), gn_g, w_out),
        in_specs=[
            pl.BlockSpec((nbatch, tc, d), lambda i: (0, i, 0)),
            _resident((1, d)),
            _resident(w_in.shape),
            chunk_spec,
            chunk_spec,
            _resident((tc, dk // 2)),
            _resident((tc, dk // 2)),
            _resident((1, vd)),
            _resident(w_out.shape),
        ],
        out_spec=pl.BlockSpec((nbatch, tc, d), lambda i: (0, i, 0)),
        out_shape=jax.ShapeDtypeStruct((nbatch, l, d), F32),
        scratch=[pltpu.VMEM((nbatch * tc, d), BF16),
                 pltpu.VMEM((nbatch * RET_HEADS, dk, dv), F32),
                 pltpu.VMEM((nbatch * tc, vd), BF16),
                 pltpu.VMEM((RET_HEADS, tc, tc), F32),
                 pltpu.VMEM((RET_HEADS, tc, LANES), F32),
                 pltpu.VMEM((RET_HEADS, tc, LANES), F32),
                 pltpu.VMEM((nbatch * tc, dv), BF16),
                 pltpu.VMEM((nbatch * tc, dv), F32)],
        semantics="arbitrary",
        name="retention_mixer",
        casts=casts,
    )


def _sconv_kernel(x_ref, g_ref, win_ref, cw_ref, wout_ref, o_ref, hn_scr, zs_scr, gy_scr):
    nbatch, tc, d = x_ref.shape
    e = cw_ref.shape[1]
    pad = SUBLANES

    @pl.when(pl.program_id(0) == 0)
    def _():
        zs_scr[:, 0:pad, :] = jnp.zeros((nbatch, pad, e), F32)

    for b in range(nbatch):
        hn_scr[b * tc:(b + 1) * tc, :] = _rms(x_ref[b], g_ref[...]).astype(BF16)
    nw = 512
    for c0 in range(0, e, nw):
        cols = slice(c0, c0 + nw)
        hn = hn_scr[...]
        gate_b = _dot(hn, win_ref[:, c0:c0 + nw])
        z = (_dot(hn, win_ref[:, e + c0:e + c0 + nw])
             * _dot(hn, win_ref[:, 2 * e + c0:2 * e + c0 + nw]))
        for b in range(nbatch):
            rows = slice(b * tc, (b + 1) * tc)
            zs_scr[b, pad:pad + tc, cols] = z[rows]
            y = (cw_ref[0:1, cols] * zs_scr[b, pad - 2:pad - 2 + tc, cols]
                 + cw_ref[1:2, cols] * zs_scr[b, pad - 1:pad - 1 + tc, cols]
                 + cw_ref[2:3, cols] * z[rows])
            zs_scr[b, 0:pad, cols] = zs_scr[b, tc:tc + pad, cols]
            gy_scr[rows, cols] = (gate_b[rows] * y).astype(BF16)
    for c0 in range(0, d, nw):
        res = _dot(gy_scr[...], wout_ref[:, c0:c0 + nw])
        for b in range(nbatch):
            o_ref[b, :, c0:c0 + nw] = x_ref[b, :, c0:c0 + nw] + res[b * tc:(b + 1) * tc]


def _sconv(x, g, w_in, conv_w, w_out, *, casts=(), tc=512):
    nbatch, l, d = x.shape
    e = conv_w.shape[1]
    tc = min(tc, l)
    return _launch(
        _sconv_kernel,
        steps=l // tc,
        inputs=(x, g, w_in, conv_w, w_out),
        in_specs=[
            pl.BlockSpec((nbatch, tc, d), lambda i: (0, i, 0)),
            _resident((1, d)),
            _resident(w_in.shape),
            _resident(conv_w.shape),
            _resident(w_out.shape),
        ],
        out_spec=pl.BlockSpec((nbatch, tc, d), lambda i: (0, i, 0)),
        out_shape=jax.ShapeDtypeStruct((nbatch, l, d), F32),
        scratch=[pltpu.VMEM((nbatch * tc, d), BF16),
                 pltpu.VMEM((nbatch, tc + SUBLANES, e), F32),
                 pltpu.VMEM((nbatch * tc, e), BF16)],
        semantics="arbitrary",
        name="sconv_mixer",
        casts=casts,
    )


def kernel(x, norm1_g, norm2_g, mlp_w1, mlp_w2, final_g, a_w_in, a_ln_g, a_ws, a_bs, a_w_out, b_w_in, b_log_dt, b_a_re, b_a_im, b_b_re, b_b_im, b_c_re, b_c_im, b_d, b_w_glu, c_w_in, c_gn_g, c_w_out, d_w_in, d_conv_w, d_w_out):
    nbatch, l, d = x.shape
    depth = norm1_g.shape[0]
    n_mixers = 4
    row = lambda v: v.reshape(1, -1).astype(F32)

    def mixer_weights(i):
        m, j = i % n_mixers, i // n_mixers
        pairs = ((a_w_in, a_w_out), (b_w_in, b_w_glu), (c_w_in, c_w_out), (d_w_in, d_w_out))[m]
        return [(w, j) for w in pairs]

    w_in, w_out = (w[j].astype(BF16) for w, j in mixer_weights(0))
    for i in range(depth):
        m, j = i % n_mixers, i // n_mixers
        g1 = row(norm1_g[i])
        mlp_casts = [(mlp_w1, i), (mlp_w2, i)]
        if m == 0:
            gd = a_ln_g.shape[1] // GM_GROUPS
            bs_b = jnp.broadcast_to(a_bs[j].astype(F32)[:, :, None], (GM_GROUPS, CHUNK, gd))
            x, (w1, w2) = _gmlp(x.reshape(nbatch * l, d), g1, w_in, row(a_ln_g[j]),
                                a_ws[j].astype(F32), bs_b, w_out, casts=mlp_casts)
        elif m == 1:
            wb, a_tiles, wc = _s5_prepare(b_log_dt[j], b_a_re[j], b_a_im[j], b_b_re[j],
                                          b_b_im[j], b_c_re[j], b_c_im[j])
            x, (w1, w2) = _s5(x, g1, w_in, wb, a_tiles, wc, row(b_d[j]), w_out, casts=mlp_casts)
        elif m == 2:
            x, (w1, w2) = _retention(x, g1, w_in, row(c_gn_g[j]), w_out, casts=mlp_casts)
        else:
            x, (w1, w2) = _sconv(x, g1, w_in, d_conv_w[j].reshape(CONV_WIDTH, -1).astype(F32),
                                 w_out, casts=mlp_casts)
        last = i == depth - 1
        x, nxt = _mlp(x.reshape(nbatch * l, d), row(norm2_g[i]), w1, w2, row(final_g),
                      final_norm=last, casts=() if last else mixer_weights(i + 1))
        x = x.reshape(nbatch, l, d)
        if not last:
            w_in, w_out = nxt
    return x
```
